```python
import math
import jax, jax.numpy as jnp
from jax import lax
import numpy as np

D_MODEL = 1024
BATCH = 32
SEQ = 2048
DEPTH = 2
DEC_BATCH = 2
DEC_SEQ = 8192
PAST_LEN = 128

GRID_W = 64
HEAD_DIM = 64
NA_HEADS = 8
NA_MAX_KH = 8
NA_KW = 16
NA_QB = NA_KW
NA_KB = 2 * NA_KW
GQA_Q_HEADS = 8
GQA_KV_HEADS = 2
DIFF_HEADS = 4
DIFF_VDIM = 2 * HEAD_DIM
N_BRANCHES = 3
BRANCH_WIDTH = 512
Q_BLOCK = 128
ROPE_THETA = 10000.0
RMS_EPS = 1e-6
D_FF = ((8 * D_MODEL // 3 + 255) // 256) * 256
NA_W = NA_HEADS * HEAD_DIM
GQA_QW = GQA_Q_HEADS * HEAD_DIM
GQA_KVW = GQA_KV_HEADS * HEAD_DIM
DIFF_QKW = DIFF_HEADS * 2 * HEAD_DIM
DIFF_VW = DIFF_HEADS * DIFF_VDIM
SPLITS = (NA_W, NA_W, NA_W, GQA_QW, GQA_KVW, GQA_KVW, DIFF_QKW, DIFF_QKW, DIFF_VW, N_BRANCHES * D_MODEL)
D_IN = sum(SPLITS)

kernel_name = "hybrid_na_gqa_diff_encoder"


def rms_norm(x, g):
    xf = x.astype(jnp.float32)
    y = xf * lax.rsqrt(jnp.mean(xf * xf, axis=-1, keepdims=True) + RMS_EPS)
    return (y * g.astype(jnp.float32)).astype(x.dtype)


def rope_cos_sin(pos, dim):
    inv = 1.0 / (ROPE_THETA ** (jnp.arange(0, dim, 2, dtype=jnp.float32) / dim))
    ang = pos.astype(jnp.float32)[:, None] * inv[None, :]
    return jnp.cos(ang), jnp.sin(ang)


def apply_rope(x, cos, sin):
    shape = (x.shape[1],) + (1,) * (x.ndim - 3) + (cos.shape[-1],)
    c = cos.reshape(shape)
    s = sin.reshape(shape)
    xf = x.astype(jnp.float32)
    x1, x2 = jnp.split(xf, 2, axis=-1)
    return jnp.concatenate([x1 * c - x2 * s, x2 * c + x1 * s], axis=-1).astype(x.dtype)


def neighbourhood_attention(q, k, v, rpb):
    B, N, H, d = q.shape
    rows = N // GRID_W
    kh = min(NA_MAX_KH, rows)
    ncb = GRID_W // NA_QB
    q = q.reshape(B, rows, GRID_W, H, d)
    k = k.reshape(B, rows, GRID_W, H, d)
    v = v.reshape(B, rows, GRID_W, H, d)
    j = np.arange(ncb)
    k_start = np.clip(j * NA_QB - NA_KW // 2, 0, GRID_W - NA_KB)
    col_idx = k_start[:, None] + np.arange(NA_KB)
    q_col = j[:, None] * NA_QB + np.arange(NA_QB)
    w_start = np.clip(q_col - NA_KW // 2, 0, GRID_W - NA_KW)
    kc = col_idx[:, None, :]
    valid = (kc >= w_start[..., None]) & (kc < w_start[..., None] + NA_KW)
    dc_idx = np.clip(kc - q_col[..., None], -(NA_KW - 1), NA_KW - 1) + NA_KW - 1
    rpb_cols = rpb[:, :, dc_idx]
    scale = d ** -0.5

    def one_row(r):
        r_start = jnp.clip(r - kh // 2, 0, rows - kh)
        q_r = lax.dynamic_index_in_dim(q, r, axis=1, keepdims=False).reshape(B, ncb, NA_QB, H, d)
        k_r = lax.dynamic_slice_in_dim(k, r_start, kh, axis=1)[:, :, col_idx]
        v_r = lax.dynamic_slice_in_dim(v, r_start, kh, axis=1)[:, :, col_idx]
        dr_idx = r_start + jnp.arange(kh) - r + NA_MAX_KH - 1
        bias = jnp.take(rpb_cols, dr_idx, axis=1).transpose(0, 2, 3, 1, 4)
        s = jnp.einsum('bjqhd,brjkhd->bhjqrk', q_r, k_r).astype(jnp.float32) * scale
        s = s + bias.astype(jnp.float32)
        s = jnp.where(valid[:, :, None, :], s, -jnp.inf)
        p = jax.nn.softmax(s.reshape(B, H, ncb, NA_QB, kh * NA_KB), axis=-1)
        p = p.reshape(B, H, ncb, NA_QB, kh, NA_KB).astype(v.dtype)
        o = jnp.einsum('bhjqrk,brjkhd->bjqhd', p, v_r)
        return o.reshape(B, GRID_W, H, d)

    out = lax.map(one_row, jnp.arange(rows))
    return out.transpose(1, 0, 2, 3, 4).reshape(B, N, H * d)


def gqa_axial_attention(q, k, v, g_q, g_k):
    B, N, _ = q.shape
    G = GQA_Q_HEADS // GQA_KV_HEADS
    q = rms_norm(q.reshape(B, N, GQA_Q_HEADS, HEAD_DIM), g_q)
    k = rms_norm(k.reshape(B, N, GQA_KV_HEADS, HEAD_DIM), g_k)
    v = v.reshape(B, N, GQA_KV_HEADS, HEAD_DIM)
    t = jnp.arange(N)
    half = HEAD_DIM // 2
    cr, sr = rope_cos_sin(t // GRID_W, half)
    cc, sc = rope_cos_sin(t % GRID_W, half)

    def axial(x):
        return jnp.concatenate([apply_rope(x[..., :half], cr, sr), apply_rope(x[..., half:], cc, sc)], axis=-1)

    q = axial(q)
    k = axial(k)
    nb = N // Q_BLOCK
    qb = q.reshape(B, nb, Q_BLOCK, GQA_KV_HEADS, G, HEAD_DIM).transpose(1, 0, 2, 3, 4, 5)
    scale = HEAD_DIM ** -0.5

    def block(qi):
        s = jnp.einsum('bqhgd,bkhd->bhgqk', qi, k).astype(jnp.float32) * scale
        p = jax.nn.softmax(s, axis=-1).astype(v.dtype)
        return jnp.einsum('bhgqk,bkhd->bqhgd', p, v)

    o = lax.map(block, qb)
    return o.transpose(1, 0, 2, 3, 4, 5).reshape(B, N, GQA_Q_HEADS * HEAD_DIM)


def diff_attention(q, k, v, lam_params, subln_g, lam_init):
    B, N, _ = q.shape
    q = q.reshape(B, N, DIFF_HEADS, 2, HEAD_DIM)
    k = k.reshape(B, N, DIFF_HEADS, 2, HEAD_DIM)
    v = v.reshape(B, N, DIFF_HEADS, DIFF_VDIM)
    cos, sin = rope_cos_sin(jnp.arange(N), HEAD_DIM)
    q = apply_rope(q, cos, sin)
    k = apply_rope(k, cos, sin)
    lp = lam_params.astype(jnp.float32)
    lam = jnp.exp(jnp.sum(lp[0] * lp[1])) - jnp.exp(jnp.sum(lp[2] * lp[3])) + lam_init
    nb = N // Q_BLOCK
    qb = q.reshape(B, nb, Q_BLOCK, DIFF_HEADS, 2, HEAD_DIM).transpose(1, 0, 2, 3, 4, 5)
    scale = HEAD_DIM ** -0.5

    def block(qi):
        s = jnp.einsum('bqhpd,bkhpd->bhpqk', qi, k).astype(jnp.float32) * scale
        p = jax.nn.softmax(s, axis=-1)
        a = (p[:, :, 0] - lam * p[:, :, 1]).astype(v.dtype)
        return jnp.einsum('bhqk,bkhe->bqhe', a, v)

    o = lax.map(block, qb).transpose(1, 0, 2, 3, 4).reshape(B, N, DIFF_HEADS, DIFF_VDIM)
    o = rms_norm(o, subln_g) * (1.0 - lam_init)
    return o.reshape(B, N, DIFF_VW)


def mixer(h, lam_init, w_in, na_rpb, qk_norm, diff_lambda, diff_subln, w_branch, w_out):
    B, N, _ = h.shape
    proj = jnp.einsum('bnd,de->bne', h, w_in)
    points = [int(p) for p in np.cumsum(SPLITS)[:-1]]
    qa, ka, va, qb, kb, vb, qc, kc, vc, gates = jnp.split(proj, points, axis=-1)
    o_a = neighbourhood_attention(qa.reshape(B, N, NA_HEADS, HEAD_DIM), ka.reshape(B, N, NA_HEADS, HEAD_DIM),
                                  va.reshape(B, N, NA_HEADS, HEAD_DIM), na_rpb)
    o_b = gqa_axial_attention(qb, kb, vb, qk_norm[0], qk_norm[1])
    o_c = diff_attention(qc, kc, vc, diff_lambda, diff_subln, lam_init)
    g = jax.nn.sigmoid(gates.astype(jnp.float32)).astype(h.dtype).reshape(B, N, N_BRANCHES, D_MODEL)
    merged = (g[:, :, 0] * jnp.einsum('bne,ed->bnd', o_a, w_branch[0])
              + g[:, :, 1] * jnp.einsum('bne,ed->bnd', o_b, w_branch[1])
              + g[:, :, 2] * jnp.einsum('bne,ed->bnd', o_c, w_branch[2]))
    return jnp.einsum('bnd,de->bne', merged, w_out)


def swiglu(h, w_gate, w_up, w_down):
    a = jnp.einsum('bnd,df->bnf', h, w_gate)
    b = jnp.einsum('bnd,df->bnf', h, w_up)
    return jnp.einsum('bnf,fd->bnd', jax.nn.silu(a) * b, w_down)


def trunk(x, norm_mix, w_in, na_rpb, qk_norm, diff_lambda, diff_subln, w_branch, w_out,
          norm_ffn, w_ffn_gate, w_ffn_up, w_ffn_down, norm_final):
    for l in range(DEPTH):
        lam_init = 0.8 - 0.6 * math.exp(-0.3 * l)
        h = rms_norm(x, norm_mix[l])
        x = x + mixer(h, lam_init, w_in[l], na_rpb[l], qk_norm[l], diff_lambda[l], diff_subln[l],
                      w_branch[l], w_out[l])
        h = rms_norm(x, norm_ffn[l])
        x = x + swiglu(h, w_ffn_gate[l], w_ffn_up[l], w_ffn_down[l])
    return rms_norm(x, norm_final)


def setup_inputs(seed: int = 0) -> dict:
    key = jax.random.key(seed)
    ks = jax.random.split(key, 16)
    f32 = jnp.float32
    nrm = lambda k, shape, s: jax.random.normal(k, shape, f32) * s
    return {
        "x_prompt": nrm(ks[0], (BATCH, SEQ, D_MODEL), 1.0),
        "x_sample": nrm(ks[1], (DEC_BATCH, DEC_SEQ, D_MODEL), 1.0),
        "norm_mix": 1.0 + nrm(ks[2], (DEPTH, D_MODEL), 0.05),
        "w_in": nrm(ks[3], (DEPTH, D_MODEL, D_IN), D_MODEL ** -0.5),
        "na_rpb": nrm(ks[4], (DEPTH, NA_HEADS, 2 * NA_MAX_KH - 1, 2 * NA_KW - 1), 0.1),
        "qk_norm": 1.0 + nrm(ks[5], (DEPTH, 2, HEAD_DIM), 0.05),
        "diff_lambda": nrm(ks[6], (DEPTH, 4, HEAD_DIM), 0.1),
        "diff_subln": 1.0 + nrm(ks[7], (DEPTH, DIFF_VDIM), 0.05),
        "w_branch": nrm(ks[8], (DEPTH, N_BRANCHES, BRANCH_WIDTH, D_MODEL), BRANCH_WIDTH ** -0.5),
        "w_out": nrm(ks[9], (DEPTH, D_MODEL, D_MODEL), D_MODEL ** -0.5),
        "norm_ffn": 1.0 + nrm(ks[10], (DEPTH, D_MODEL), 0.05),
        "w_ffn_gate": nrm(ks[11], (DEPTH, D_MODEL, D_FF), D_MODEL ** -0.5),
        "w_ffn_up": nrm(ks[12], (DEPTH, D_MODEL, D_FF), D_MODEL ** -0.5),
        "w_ffn_down": nrm(ks[13], (DEPTH, D_FF, D_MODEL), D_FF ** -0.5),
        "norm_final": 1.0 + nrm(ks[14], (D_MODEL,), 0.05),
    }


def reference(x_prompt, x_sample, norm_mix, w_in, na_rpb, qk_norm, diff_lambda, diff_subln, w_branch,
              w_out, norm_ffn, w_ffn_gate, w_ffn_up, w_ffn_down, norm_final):
    y_prompt = trunk(x_prompt, norm_mix, w_in, na_rpb, qk_norm, diff_lambda, diff_subln, w_branch, w_out,
                     norm_ffn, w_ffn_gate, w_ffn_up, w_ffn_down, norm_final)
    y_sample = trunk(x_sample, norm_mix, w_in, na_rpb, qk_norm, diff_lambda, diff_subln, w_branch, w_out,
                     norm_ffn, w_ffn_gate, w_ffn_up, w_ffn_down, norm_final)
    return (y_prompt, y_sample)
```

```python
import functools
import math

import jax
import jax.numpy as jnp
import numpy as np
from jax import lax
from jax.experimental import pallas as pl
from jax.experimental.pallas import tpu as pltpu

F32 = jnp.float32
BF16 = jnp.bfloat16

D_MODEL = 1024
GRID_W = 64
HEAD_DIM = 64
NA_HEADS = 8
NA_MAX_KH = 8
NA_KW = 16
GQA_Q_HEADS = 8
GQA_KV_HEADS = 2
DIFF_HEADS = 4
DIFF_VDIM = 128
BRANCH_WIDTH = 512
ROPE_THETA = 10000.0
RMS_EPS = 1e-6
D_FF = 2816
QKV_COLS = 3840

LANES = 128
VMEM_LIMIT_BYTES = 56 * 1024 * 1024

LOG2E = 1.4426950408889634
Q_SCALE = HEAD_DIM ** -0.5 * LOG2E
NEG_BIG = -1e30

TOKEN_TILE = 512
NA_ROWS_PER_STEP = 8
GQA_Q_TILE = 512
GQA_K_TILE = 1024


def _const_spec(shape):
    nd = len(shape)
    return pl.BlockSpec(shape, lambda *_: (0,) * nd, pipeline_mode=pl.Buffered(1))


def _params(*sem):
    return pltpu.CompilerParams(dimension_semantics=sem, vmem_limit_bytes=VMEM_LIMIT_BYTES)


def _rms(x, eps=RMS_EPS):
    return x * lax.rsqrt(jnp.mean(x * x, axis=-1, keepdims=True) + eps)


def _gqa_lane(qd, i, slot):
    return (qd % 2) * 128 + slot * 32 + (qd // 2) * 16 + i


def _build_column_perm():
    cols = []
    cols += list(range(0, 1536))
    qb = np.zeros(512, np.int64)
    for h in range(GQA_Q_HEADS):
        for d in range(HEAD_DIM):
            qb[(h // 4) * 256 + _gqa_lane(d // 16, d % 16, h % 4)] = 1536 + h * 64 + d
    cols += list(qb)
    kb = np.zeros(512, np.int64)
    for v in range(GQA_KV_HEADS):
        for d in range(HEAD_DIM):
            for slot in range(4):
                kb[v * 256 + _gqa_lane(d // 16, d % 16, slot)] = 2048 + v * 64 + d
    cols += list(kb)
    v0 = list(range(2176, 2240))
    v1 = list(range(2240, 2304))
    cols += v0 + v1 + v1 + v0
    for base in (2304, 2816):
        p = np.zeros(512, np.int64)
        for j in range(2 * DIFF_HEADS):
            for d in range(HEAD_DIM):
                p[(j // 4) * 256 + (d // 32) * 128 + (j % 4) * 32 + d % 32] = base + j * 64 + d
        cols += list(p)
    cols += list(range(3328, 3840))
    return np.asarray(cols, np.int32)


_COL_PERM = _build_column_perm()
_C_QA, _C_KA, _C_VA, _C_QB, _C_KB, _C_VB, _C_QC, _C_KC, _C_VC, _C_END = (
    0, 512, 1024, 1536, 2048, 2560, 2816, 3328, 3840, 4352)


def _gqa_gain_tiles(g):
    idx = np.zeros((2, 128), np.int32)
    for slot in range(4):
        for ab in range(2):
            for i in range(16):
                idx[0, slot * 32 + ab * 16 + i] = ab * 32 + i
                idx[1, slot * 32 + ab * 16 + i] = ab * 32 + 16 + i
    return g.astype(F32)[idx]


def _rope_tables(n):
    t = jnp.arange(n)
    inv32 = 1.0 / (ROPE_THETA ** (jnp.arange(0, 32, 2, dtype=F32) / 32))
    inv64 = 1.0 / (ROPE_THETA ** (jnp.arange(0, 64, 2, dtype=F32) / 64))
    ang_r = (t // GRID_W).astype(F32)[:, None] * inv32[None, :]
    ang_c = (t % GRID_W).astype(F32)[:, None] * inv32[None, :]
    ang_g = jnp.tile(jnp.concatenate([ang_r, ang_c], axis=-1), (1, 4))
    ang_d = jnp.tile(t.astype(F32)[:, None] * inv64[None, :], (1, 4))
    return jnp.cos(ang_g), jnp.sin(ang_g), jnp.cos(ang_d), jnp.sin(ang_d)


def _inproj_kernel(x_ref, g_ref, w_ref, cg_ref, sg_ref, cd_ref, sd_ref, gq_ref, gk_ref,
                   qa_ref, ka_ref, va_ref, qb_ref, kb_ref, vb_ref, qc_ref, kc_ref, vc_ref):
    h = (_rms(x_ref[...]) * g_ref[...]).astype(BF16)

    def proj(c0, c1):
        return jnp.dot(h, w_ref[:, c0:c1], preferred_element_type=F32)

    qa_ref[...] = (proj(_C_QA, _C_KA) * Q_SCALE).astype(BF16)
    ka_ref[...] = proj(_C_KA, _C_VA).astype(BF16)
    va_ref[...] = proj(_C_VA, _C_QB).astype(BF16)
    vb_ref[...] = proj(_C_VB, _C_QC).astype(BF16)
    vc_ref[...] = proj(_C_VC, _C_END).astype(BF16)

    slot = lax.broadcasted_iota(jnp.int32, (1, LANES), 1) // 32

    def rope(x1, x2, c, s):
        return x1 * c - x2 * s, x2 * c + x1 * s

    cg, sg = cg_ref[...], sg_ref[...]

    def gqa(y, gains_ref, out_ref, per_slot, scale):
        for grp in range(2):
            p = y[:, grp * 256:grp * 256 + 128]
            q = y[:, grp * 256 + 128:grp * 256 + 256]
            ssq = p * p + q * q
            if per_slot:
                r = jnp.zeros_like(ssq)
                for sl in range(4):
                    tot = jnp.sum(jnp.where(slot == sl, ssq, 0.0), axis=-1, keepdims=True)
                    r = jnp.where(slot == sl, lax.rsqrt(tot * (1.0 / HEAD_DIM) + RMS_EPS), r)
            else:
                tot = jnp.sum(jnp.where(slot == 0, ssq, 0.0), axis=-1, keepdims=True)
                r = lax.rsqrt(tot * (1.0 / HEAD_DIM) + RMS_EPS)
            o1, o2 = rope(p * r * gains_ref[0:1, :], q * r * gains_ref[1:2, :], cg, sg)
            out_ref[:, grp * 256:grp * 256 + 128] = (o1 * scale).astype(BF16)
            out_ref[:, grp * 256 + 128:grp * 256 + 256] = (o2 * scale).astype(BF16)

    gqa(proj(_C_QB, _C_KB), gq_ref, qb_ref, True, Q_SCALE)
    gqa(proj(_C_KB, _C_VB), gk_ref, kb_ref, False, 1.0)

    cd, sd = cd_ref[...], sd_ref[...]

    def diff(y, out_ref, scale):
        for grp in range(2):
            o1, o2 = rope(y[:, grp * 256:grp * 256 + 128], y[:, grp * 256 + 128:grp * 256 + 256], cd, sd)
            out_ref[:, grp * 256:grp * 256 + 128] = (o1 * scale).astype(BF16)
            out_ref[:, grp * 256 + 128:grp * 256 + 256] = (o2 * scale).astype(BF16)

    diff(proj(_C_QC, _C_KC), qc_ref, Q_SCALE)
    diff(proj(_C_KC, _C_VC), kc_ref, 1.0)


def _inproj(x2d, n_seq, gain, w_perm, tables, gq, gk):
    t = x2d.shape[0]
    tm = TOKEN_TILE
    assert t % tm == 0 and n_seq % tm == 0
    seq_tiles = n_seq // tm
    tok = lambda c: pl.BlockSpec((tm, c), lambda i: (i, 0))
    tab = pl.BlockSpec((tm, LANES), lambda i: (i % seq_tiles, 0))
    widths = (512, 512, 512, 512, 512, 256, 512, 512, 512)
    return pl.pallas_call(
        _inproj_kernel,
        grid=(t // tm,),
        in_specs=[tok(D_MODEL), _const_spec((1, D_MODEL)), _const_spec((D_MODEL, _C_END)),
                  tab, tab, tab, tab, _const_spec((2, LANES)), _const_spec((2, LANES))],
        out_specs=[tok(c) for c in widths],
        out_shape=[jax.ShapeDtypeStruct((t, c), BF16) for c in widths],
        compiler_params=_params("parallel"),
        name="inproj",
    )(x2d, gain, w_perm, *tables, gq, gk)


def _na_bias_table(rpb):
    qc = np.arange(GRID_W)[:, None]
    kc = np.arange(GRID_W)[None, :]
    w_start = np.clip(qc - NA_KW // 2, 0, GRID_W - NA_KW)
    valid = (kc >= w_start) & (kc < w_start + NA_KW)
    dc = np.clip(kc - qc, -(NA_KW - 1), NA_KW - 1) + NA_KW - 1
    off = np.arange(NA_MAX_KH)[:, None]
    win = np.arange(NA_MAX_KH)[None, :]
    dr = win + (NA_MAX_KH - 1) - off
    b = rpb.astype(F32)[:, dr][:, :, :, dc]
    b = jnp.where(valid[None, None, None], b * LOG2E, NEG_BIG)
    b = b.transpose(1, 0, 3, 2, 4)
    return b.reshape(NA_MAX_KH, NA_HEADS, GRID_W, NA_MAX_KH * GRID_W)


def _na_kernel(q_ref, k_ref, v_ref, b_ref, o_ref, *, rows):
    c = pl.program_id(2)
    lane = lax.broadcasted_iota(jnp.int32, (1, LANES), 1)
    lo = lane < HEAD_DIM
    win_tokens = NA_MAX_KH * GRID_W

    def row(i, carry):
        r = c * NA_ROWS_PER_STEP + i
        r_start = jnp.clip(r - NA_MAX_KH // 2, 0, rows - NA_MAX_KH)
        off = r - r_start
        t0 = pl.multiple_of(r_start * GRID_W, GRID_W)
        kw = k_ref[pl.ds(t0, win_tokens), :]
        vw = v_ref[pl.ds(t0, win_tokens), :]
        q0 = pl.multiple_of(i * GRID_W, GRID_W)
        q = q_ref[pl.ds(q0, GRID_W), :]
        outs = []
        for e in range(2):
            qm = jnp.where(lo if e == 0 else ~lo, q, jnp.zeros_like(q))
            s = lax.dot_general(qm, kw, (((1,), (1,)), ((), ())), preferred_element_type=F32)
            s = s + b_ref[off, e]
            p = jnp.exp2(s - jnp.max(s, axis=-1, keepdims=True))
            l = jnp.sum(p, axis=-1, keepdims=True)
            outs.append(jnp.dot(p.astype(BF16), vw, preferred_element_type=F32) / l)
        o_ref[pl.ds(q0, GRID_W), :] = jnp.where(lo, outs[0], outs[1]).astype(BF16)
        return carry

    lax.fori_loop(0, NA_ROWS_PER_STEP, row, 0)


def _na_attention(q, k, v, bias, batch, n_seq):
    rows = n_seq // GRID_W
    assert rows >= NA_MAX_KH and rows % NA_ROWS_PER_STEP == 0
    chunk = NA_ROWS_PER_STEP * GRID_W
    chunks = n_seq // chunk
    qspec = pl.BlockSpec((chunk, LANES), lambda b, hp, c: (b * chunks + c, hp))
    kvspec = pl.BlockSpec((n_seq, LANES), lambda b, hp, c: (b, hp))
    bspec = pl.BlockSpec((NA_MAX_KH, 2, GRID_W, NA_MAX_KH * GRID_W), lambda b, hp, c: (0, hp, 0, 0))
    return pl.pallas_call(
        functools.partial(_na_kernel, rows=rows),
        grid=(batch, NA_HEADS // 2, chunks),
        in_specs=[qspec, kvspec, kvspec, bspec],
        out_specs=qspec,
        out_shape=jax.ShapeDtypeStruct(q.shape, BF16),
        compiler_params=_params("parallel", "parallel", "arbitrary"),
        name="na_attn",
    )(q, k, v, bias)


def _gqa_kernel(q_ref, k_ref, v0_ref, v1_ref, o_ref, m_sc, l_sc, acc_sc):
    hp = pl.program_id(2)
    ki = pl.program_id(3)

    @pl.when(ki == 0)
    def _():
        m_sc[...] = jnp.full_like(m_sc, -jnp.inf)
        l_sc[...] = jnp.zeros_like(l_sc)
        acc_sc[...] = jnp.zeros_like(acc_sc)

    q = q_ref[...]
    k = k_ref[...]
    slot = (lax.broadcasted_iota(jnp.int32, (1, 2 * LANES), 1) % LANES) // 32
    for e, v_ref in enumerate((v0_ref, v1_ref)):
        hh = 2 * (hp % 2) + e
        qm = jnp.where(slot == hh, q, jnp.zeros_like(q))
        s = lax.dot_general(qm, k, (((1,), (1,)), ((), ())), preferred_element_type=F32)
        m_prev = m_sc[e]
        m_new = jnp.maximum(m_prev, jnp.max(s, axis=-1, keepdims=True))
        alpha = jnp.exp2(m_prev - m_new)
        p = jnp.exp2(s - m_new)
        l_sc[e] = alpha * l_sc[e] + jnp.sum(p, axis=-1, keepdims=True)
        acc_sc[e] = alpha * acc_sc[e] + jnp.dot(p.astype(BF16), v_ref[...],
                                                preferred_element_type=F32)
        m_sc[e] = m_new

    @pl.when(ki == pl.num_programs(3) - 1)
    def _():
        lo = lax.broadcasted_iota(jnp.int32, (1, LANES), 1) < HEAD_DIM
        o_ref[...] = jnp.where(lo, acc_sc[0] / l_sc[0], acc_sc[1] / l_sc[1]).astype(BF16)


def _gqa_attention(q, k, v, batch, n_seq):
    tq = GQA_Q_TILE
    tk = min(GQA_K_TILE, n_seq)
    assert n_seq % tq == 0 and n_seq % tk == 0
    nq, nk = n_seq // tq, n_seq // tk
    return pl.pallas_call(
        _gqa_kernel,
        grid=(batch, nq, GQA_Q_HEADS // 2, nk),
        in_specs=[pl.BlockSpec((tq, 2 * LANES), lambda b, i, hp, j: (b * nq + i, hp // 2)),
                  pl.BlockSpec((tk, 2 * LANES), lambda b, i, hp, j: (b * nk + j, hp // 2)),
                  pl.BlockSpec((tk, LANES), lambda b, i, hp, j: (b * nk + j, hp // 2)),
                  pl.BlockSpec((tk, LANES), lambda b, i, hp, j: (b * nk + j, 1 - hp // 2))],
        out_specs=pl.BlockSpec((tq, LANES), lambda b, i, hp, j: (b * nq + i, hp)),
        out_shape=jax.ShapeDtypeStruct((batch * n_seq, BRANCH_WIDTH), BF16),
        scratch_shapes=[pltpu.VMEM((2, tq, 1), F32), pltpu.VMEM((2, tq, 1), F32),
                        pltpu.VMEM((2, tq, LANES), F32)],
        compiler_params=_params("parallel", "parallel", "arbitrary", "arbitrary"),
        name="gqa_attn",
    )(q, k, v, v)


def _diff_kernel(q_ref, k_ref, v_ref, lam_ref, g_ref, o_ref, *, lam_init):
    h = pl.program_id(2)
    q = q_ref[...]
    k = k_ref[...]
    slot = (lax.broadcasted_iota(jnp.int32, (1, 2 * LANES), 1) % LANES) // 32
    lp = lam_ref[...]
    lam = (jnp.exp(jnp.sum(lp[0:1] * lp[1:2], axis=-1, keepdims=True))
           - jnp.exp(jnp.sum(lp[2:3] * lp[3:4], axis=-1, keepdims=True)) + lam_init)
    es, ls = [], []
    for p_i in range(2):
        sub = 2 * (h % 2) + p_i
        qm = jnp.where(slot == sub, q, jnp.zeros_like(q))
        s = lax.dot_general(qm, k, (((1,), (1,)), ((), ())), preferred_element_type=F32)
        e = jnp.exp2(s - jnp.max(s, axis=-1, keepdims=True))
        es.append(e)
        ls.append(jnp.sum(e, axis=-1, keepdims=True))
    a = es[0] * (1.0 / ls[0]) - es[1] * (lam / ls[1])
    o = jnp.dot(a.astype(BF16), v_ref[...], preferred_element_type=F32)
    o_ref[...] = (_rms(o) * g_ref[...] * (1.0 - lam_init)).astype(BF16)


def _diff_attention(q, k, v, lam_params, subln_g, lam_init, batch, n_seq):
    tq = 512 if n_seq <= 2048 else 128
    assert n_seq % tq == 0
    nq = n_seq // tq
    return pl.pallas_call(
        functools.partial(_diff_kernel, lam_init=lam_init),
        grid=(batch, nq, DIFF_HEADS),
        in_specs=[pl.BlockSpec((tq, 2 * LANES), lambda b, i, h: (b * nq + i, h // 2)),
                  pl.BlockSpec((n_seq, 2 * LANES), lambda b, i, h: (b, h // 2)),
                  pl.BlockSpec((n_seq, DIFF_VDIM), lambda b, i, h: (b, h)),
                  _const_spec((4, HEAD_DIM)), _const_spec((1, DIFF_VDIM))],
        out_specs=pl.BlockSpec((tq, DIFF_VDIM), lambda b, i, h: (b * nq + i, h)),
        out_shape=jax.ShapeDtypeStruct((batch * n_seq, BRANCH_WIDTH), BF16),
        compiler_params=_params("parallel", "parallel", "arbitrary"),
        name="diff_attn",
    )(q, k, v, lam_params, subln_g)


def _merge_kernel(x_ref, g_ref, oa_ref, ob_ref, oc_ref, wg_ref, wb_ref, wo_ref, y_ref):
    x = x_ref[...]
    h = (_rms(x) * g_ref[...]).astype(BF16)
    merged = None
    for i, o_ref in enumerate((oa_ref, ob_ref, oc_ref)):
        gate = jax.nn.sigmoid(jnp.dot(h, wg_ref[:, i * D_MODEL:(i + 1) * D_MODEL],
                                      preferred_element_type=F32))
        term = gate * jnp.dot(o_ref[...], wb_ref[i], preferred_element_type=F32)
        merged = term if merged is None else merged + term
    y_ref[...] = x + jnp.dot(merged.astype(BF16), wo_ref[...], preferred_element_type=F32)


def _merge(x2d, gain, oa, ob, oc, w_gate, w_branch, w_out):
    t = x2d.shape[0]
    tm = TOKEN_TILE
    tok = lambda c: pl.BlockSpec((tm, c), lambda i: (i, 0))
    return pl.pallas_call(
        _merge_kernel,
        grid=(t // tm,),
        in_specs=[tok(D_MODEL), _const_spec((1, D_MODEL)), tok(BRANCH_WIDTH), tok(BRANCH_WIDTH),
                  tok(BRANCH_WIDTH), _const_spec((D_MODEL, 3 * D_MODEL)),
                  _const_spec((3, BRANCH_WIDTH, D_MODEL)), _const_spec((D_MODEL, D_MODEL))],
        out_specs=tok(D_MODEL),
        out_shape=jax.ShapeDtypeStruct((t, D_MODEL), F32),
        compiler_params=_params("parallel"),
        name="merge",
    )(x2d, gain, oa, ob, oc, w_gate, w_branch, w_out)


FFN_CHUNKS = 2


def _ffn_kernel(x_ref, g_ref, wg_ref, wu_ref, wd_ref, gf_ref, y_ref, *, final_norm):
    x = x_ref[...]
    h = (_rms(x) * g_ref[...]).astype(BF16)
    width = D_FF // FFN_CHUNKS
    y = x
    for c in range(FFN_CHUNKS):
        cs = slice(c * width, (c + 1) * width)
        a = jnp.dot(h, wg_ref[:, cs], preferred_element_type=F32)
        b = jnp.dot(h, wu_ref[:, cs], preferred_element_type=F32)
        act = (a * jax.nn.sigmoid(a) * b).astype(BF16)
        y = y + jnp.dot(act, wd_ref[cs, :], preferred_element_type=F32)
    if final_norm:
        y = _rms(y) * gf_ref[...]
    y_ref[...] = y


def _ffn(x2d, gain, w_gate, w_up, w_down, gain_final, final_norm):
    t = x2d.shape[0]
    tm = TOKEN_TILE
    tok = pl.BlockSpec((tm, D_MODEL), lambda i: (i, 0))
    return pl.pallas_call(
        functools.partial(_ffn_kernel, final_norm=final_norm),
        grid=(t // tm,),
        in_specs=[tok, _const_spec((1, D_MODEL)), _const_spec((D_MODEL, D_FF)),
                  _const_spec((D_MODEL, D_FF)), _const_spec((D_FF, D_MODEL)),
                  _const_spec((1, D_MODEL))],
        out_specs=tok,
        out_shape=jax.ShapeDtypeStruct((t, D_MODEL), F32),
        compiler_params=_params("parallel"),
        name="ffn",
    )(x2d, gain, w_gate, w_up, w_down, gain_final)


def _prepare_layer(l, norm_mix, w_in, na_rpb, qk_norm, diff_lambda, diff_subln, w_branch, w_out,
                   norm_ffn, w_ffn_gate, w_ffn_up, w_ffn_down):
    row = lambda g: g.astype(F32).reshape(1, -1)
    return dict(
        lam_init=0.8 - 0.6 * math.exp(-0.3 * l),
        norm_mix=row(norm_mix[l]),
        w_qkv=w_in[l][:, _COL_PERM].astype(BF16),
        w_gate=w_in[l][:, QKV_COLS:].astype(BF16),
        na_bias=_na_bias_table(na_rpb[l]),
        gq=_gqa_gain_tiles(qk_norm[l, 0]),
        gk=_gqa_gain_tiles(qk_norm[l, 1]),
        diff_lambda=diff_lambda[l].astype(F32),
        diff_subln=row(diff_subln[l]),
        w_branch=w_branch[l].astype(BF16),
        w_out=w_out[l].astype(BF16),
        norm_ffn=row(norm_ffn[l]),
        w_ffn_gate=w_ffn_gate[l].astype(BF16),
        w_ffn_up=w_ffn_up[l].astype(BF16),
        w_ffn_down=w_ffn_down[l].astype(BF16),
    )


def _trunk(x, layers, norm_final):
    batch, n_seq, _ = x.shape
    x2d = x.reshape(batch * n_seq, D_MODEL)
    tables = _rope_tables(n_seq)
    gf = norm_final.astype(F32).reshape(1, -1)
    for l, p in enumerate(layers):
        qa, ka, va, qb, kb, vb, qc, kc, vc = _inproj(x2d, n_seq, p["norm_mix"], p["w_qkv"], tables,
                                                     p["gq"], p["gk"])
        o_a = _na_attention(qa, ka, va, p["na_bias"], batch, n_seq)
        o_b = _gqa_attention(qb, kb, vb, batch, n_seq)
        o_c = _diff_attention(qc, kc, vc, p["diff_lambda"], p["diff_subln"], p["lam_init"], batch, n_seq)
        x2d = _merge(x2d, p["norm_mix"], o_a, o_b, o_c, p["w_gate"], p["w_branch"], p["w_out"])
        x2d = _ffn(x2d, p["norm_ffn"], p["w_ffn_gate"], p["w_ffn_up"], p["w_ffn_down"], gf,
                   final_norm=(l == len(layers) - 1))
    return x2d.reshape(batch, n_seq, D_MODEL)


def kernel(x_prompt, x_sample, norm_mix, w_in, na_rpb, qk_norm, diff_lambda, diff_subln, w_branch,
           w_out, norm_ffn, w_ffn_gate, w_ffn_up, w_ffn_down, norm_final):
    depth = w_in.shape[0]
    layers = [_prepare_layer(l, norm_mix, w_in, na_rpb, qk_norm, diff_lambda, diff_subln, w_branch,
                             w_out, norm_ffn, w_ffn_gate, w_ffn_up, w_ffn_down) for l in range(depth)]
    return (_trunk(x_prompt, layers, norm_final), _trunk(x_sample, layers, norm_final))
```

```python
import functools
import math

import jax
import jax.numpy as jnp
import numpy as np
from jax import lax
from jax.experimental import pallas as pl
from jax.experimental.pallas import tpu as pltpu

F32 = jnp.float32
BF16 = jnp.bfloat16

D_MODEL = 1024
GRID_W = 64
HEAD_DIM = 64
NA_HEADS = 8
NA_MAX_KH = 8
NA_KW = 16
GQA_Q_HEADS = 8
GQA_KV_HEADS = 2
DIFF_HEADS = 4
DIFF_VDIM = 128
BRANCH_WIDTH = 512
ROPE_THETA = 10000.0
RMS_EPS = 1e-6
D_FF = 2816
QKV_COLS = 3840

LANES = 128
VMEM_LIMIT_BYTES = 56 * 1024 * 1024

LOG2E = 1.4426950408889634
Q_SCALE = HEAD_DIM ** -0.5 * LOG2E
NEG_BIG = -1e30

TOKEN_TILE = 512
NA_ROWS_PER_STEP = 8
GQA_Q_TILE = 512
DIFF_Q_TILE = 512
CHUNKS_PER_ITER = 1
SAFE_EXCESS = 64.0
PROBE_KEYS = 128


def _const_spec(shape):
    nd = len(shape)
    return pl.BlockSpec(shape, lambda *_: (0,) * nd, pipeline_mode=pl.Buffered(1))


def _params(*sem):
    return pltpu.CompilerParams(dimension_semantics=sem, vmem_limit_bytes=VMEM_LIMIT_BYTES)


def _rms(x, eps=RMS_EPS):
    return x * lax.rsqrt(jnp.mean(x * x, axis=-1, keepdims=True) + eps)


def _gqa_lane(qd, i, slot):
    return (qd % 2) * 128 + slot * 32 + (qd // 2) * 16 + i


def _build_column_perm():
    cols = []
    cols += list(range(0, 1536))
    qb = np.zeros(512, np.int64)
    for h in range(GQA_Q_HEADS):
        for d in range(HEAD_DIM):
            qb[(h // 4) * 256 + _gqa_lane(d // 16, d % 16, h % 4)] = 1536 + h * 64 + d
    cols += list(qb)
    kb = np.zeros(512, np.int64)
    for v in range(GQA_KV_HEADS):
        for d in range(HEAD_DIM):
            for slot in range(4):
                kb[v * 256 + _gqa_lane(d // 16, d % 16, slot)] = 2048 + v * 64 + d
    cols += list(kb)
    cols += list(range(2176, 2304))
    for base in (2304, 2816):
        p = np.zeros(512, np.int64)
        for j in range(2 * DIFF_HEADS):
            for d in range(HEAD_DIM):
                p[(j // 4) * 256 + (d // 32) * 128 + (j % 4) * 32 + d % 32] = base + j * 64 + d
        cols += list(p)
    cols += list(range(3328, 3840))
    return np.asarray(cols, np.int32)


_COL_PERM = _build_column_perm()
_C_QA, _C_KA, _C_VA, _C_QB, _C_KB, _C_VB, _C_QC, _C_KC, _C_VC, _C_END = (
    0, 512, 1024, 1536, 2048, 2560, 2688, 3200, 3712, 4224)


def _gqa_gain_tiles(g):
    idx = np.zeros((2, 128), np.int32)
    for slot in range(4):
        for ab in range(2):
            for i in range(16):
                idx[0, slot * 32 + ab * 16 + i] = ab * 32 + i
                idx[1, slot * 32 + ab * 16 + i] = ab * 32 + 16 + i
    return g.astype(F32)[idx]


def _rope_tables(n):
    t = jnp.arange(n)
    inv32 = 1.0 / (ROPE_THETA ** (jnp.arange(0, 32, 2, dtype=F32) / 32))
    inv64 = 1.0 / (ROPE_THETA ** (jnp.arange(0, 64, 2, dtype=F32) / 64))
    ang_r = (t // GRID_W).astype(F32)[:, None] * inv32[None, :]
    ang_c = (t % GRID_W).astype(F32)[:, None] * inv32[None, :]
    ang_g = jnp.tile(jnp.concatenate([ang_r, ang_c], axis=-1), (1, 4))
    ang_d = jnp.tile(t.astype(F32)[:, None] * inv64[None, :], (1, 4))
    return jnp.cos(ang_g), jnp.sin(ang_g), jnp.cos(ang_d), jnp.sin(ang_d)


def _inproj_kernel(x_ref, g_ref, w_ref, cg_ref, sg_ref, cd_ref, sd_ref, gq_ref, gk_ref,
                   qa_ref, ka_ref, va_ref, qb_ref, kb_ref, vb_ref, qc_ref, kc_ref, vc_ref):
    h = (_rms(x_ref[...]) * g_ref[...]).astype(BF16)

    def proj(c0, c1):
        return jnp.dot(h, w_ref[:, c0:c1], preferred_element_type=F32)

    qa_ref[...] = (proj(_C_QA, _C_KA) * Q_SCALE).astype(BF16)
    ka_ref[...] = proj(_C_KA, _C_VA).astype(BF16)
    va_ref[...] = proj(_C_VA, _C_QB).astype(BF16)
    vt = proj(_C_VB, _C_QC).T
    ones = jnp.ones((HEAD_DIM, vt.shape[1]), F32)
    vb_ref[0] = jnp.concatenate([vt[0:HEAD_DIM], ones, vt[HEAD_DIM:], ones], axis=0).astype(BF16)
    vc_ref[0] = proj(_C_VC, _C_END).T.astype(BF16)

    slot = lax.broadcasted_iota(jnp.int32, (1, LANES), 1) // 32

    def rope(x1, x2, c, s):
        return x1 * c - x2 * s, x2 * c + x1 * s

    cg, sg = cg_ref[...], sg_ref[...]

    def gqa(y, gains_ref, out_ref, per_slot, scale):
        for grp in range(2):
            p = y[:, grp * 256:grp * 256 + 128]
            q = y[:, grp * 256 + 128:grp * 256 + 256]
            ssq = p * p + q * q
            if per_slot:
                r = jnp.zeros_like(ssq)
                for sl in range(4):
                    tot = jnp.sum(jnp.where(slot == sl, ssq, 0.0), axis=-1, keepdims=True)
                    r = jnp.where(slot == sl, lax.rsqrt(tot * (1.0 / HEAD_DIM) + RMS_EPS), r)
            else:
                tot = jnp.sum(jnp.where(slot == 0, ssq, 0.0), axis=-1, keepdims=True)
                r = lax.rsqrt(tot * (1.0 / HEAD_DIM) + RMS_EPS)
            o1, o2 = rope(p * r * gains_ref[0:1, :], q * r * gains_ref[1:2, :], cg, sg)
            out_ref[:, grp * 256:grp * 256 + 128] = (o1 * scale).astype(BF16)
            out_ref[:, grp * 256 + 128:grp * 256 + 256] = (o2 * scale).astype(BF16)

    gqa(proj(_C_QB, _C_KB), gq_ref, qb_ref, True, Q_SCALE)
    gqa(proj(_C_KB, _C_VB), gk_ref, kb_ref, False, 1.0)

    cd, sd = cd_ref[...], sd_ref[...]

    def diff(y, out_ref, scale):
        for grp in range(2):
            o1, o2 = rope(y[:, grp * 256:grp * 256 + 128], y[:, grp * 256 + 128:grp * 256 + 256], cd, sd)
            out_ref[:, grp * 256:grp * 256 + 128] = (o1 * scale).astype(BF16)
            out_ref[:, grp * 256 + 128:grp * 256 + 256] = (o2 * scale).astype(BF16)

    diff(proj(_C_QC, _C_KC), qc_ref, Q_SCALE)
    diff(proj(_C_KC, _C_VC), kc_ref, 1.0)


def _inproj(x2d, n_seq, gain, w_perm, tables, gq, gk):
    t = x2d.shape[0]
    tm = TOKEN_TILE
    assert t % tm == 0 and n_seq % tm == 0
    seq_tiles = n_seq // tm
    tok = lambda c: pl.BlockSpec((tm, c), lambda i: (i, 0))
    tab = pl.BlockSpec((tm, LANES), lambda i: (i % seq_tiles, 0))
    widths = (512, 512, 512, 512, 512, -4 * HEAD_DIM, 512, 512, -BRANCH_WIDTH)
    spec = lambda c: tok(c) if c > 0 else pl.BlockSpec((1, -c, tm), lambda i: (i, 0, 0))
    shape = lambda c: jax.ShapeDtypeStruct((t, c) if c > 0 else (t // tm, -c, tm), BF16)
    return pl.pallas_call(
        _inproj_kernel,
        grid=(t // tm,),
        in_specs=[tok(D_MODEL), _const_spec((1, D_MODEL)), _const_spec((D_MODEL, _C_END)),
                  tab, tab, tab, tab, _const_spec((2, LANES)), _const_spec((2, LANES))],
        out_specs=[spec(c) for c in widths],
        out_shape=[shape(c) for c in widths],
        compiler_params=_params("parallel"),
        name="inproj",
    )(x2d, gain, w_perm, *tables, gq, gk)


def _na_bias_table(rpb):
    qc = np.arange(GRID_W)[:, None]
    kc = np.arange(GRID_W)[None, :]
    w_start = np.clip(qc - NA_KW // 2, 0, GRID_W - NA_KW)
    valid = (kc >= w_start) & (kc < w_start + NA_KW)
    dc = np.clip(kc - qc, -(NA_KW - 1), NA_KW - 1) + NA_KW - 1
    off = np.arange(NA_MAX_KH)[:, None]
    win = np.arange(NA_MAX_KH)[None, :]
    dr = win + (NA_MAX_KH - 1) - off
    b = rpb.astype(F32)[:, dr][:, :, :, dc]
    b = jnp.where(valid[None, None, None], b * LOG2E, NEG_BIG)
    b = b.reshape(NA_HEADS // 2, 2, NA_MAX_KH, NA_MAX_KH, GRID_W, GRID_W)
    b = b.transpose(2, 0, 3, 5, 1, 4)
    return b.reshape(NA_MAX_KH, NA_HEADS // 2, NA_MAX_KH * GRID_W, 2 * GRID_W)


def _na_kernel(q_ref, k_ref, v_ref, b_ref, o_ref, *, rows):
    c = pl.program_id(2)
    lo = lax.broadcasted_iota(jnp.int32, (1, LANES), 1) < HEAD_DIM
    win_tokens = NA_MAX_KH * GRID_W

    def window(i):
        r = c * NA_ROWS_PER_STEP + i
        r_start = jnp.clip(r - NA_MAX_KH // 2, 0, rows - NA_MAX_KH)
        return r - r_start, pl.multiple_of(r_start * GRID_W, GRID_W)

    def scores(i):
        _, t0 = window(i)
        q = q_ref[i * GRID_W:(i + 1) * GRID_W, :]
        zero = jnp.zeros_like(q)
        qq = jnp.concatenate([jnp.where(lo, q, zero), jnp.where(lo, zero, q)], axis=0)
        return _nt_dot(k_ref[pl.ds(t0, win_tokens), :], qq)

    def attend(i, st):
        off, t0 = window(i)
        st = st + b_ref[off, 0]
        p = jnp.exp2(st - jnp.max(st, axis=0, keepdims=True))
        p = p * (1.0 / jnp.sum(p, axis=0, keepdims=True))
        o2 = jnp.dot(p.T.astype(BF16), v_ref[pl.ds(t0, win_tokens), :],
                     preferred_element_type=F32)
        o_ref[i * GRID_W:(i + 1) * GRID_W, :] = jnp.where(lo, o2[0:GRID_W], o2[GRID_W:]).astype(BF16)

    _skewed(list(range(NA_ROWS_PER_STEP)), scores, attend)


def _na_attention(q, k, v, bias, batch, n_seq):
    rows = n_seq // GRID_W
    assert rows >= NA_MAX_KH and rows % NA_ROWS_PER_STEP == 0
    chunk = NA_ROWS_PER_STEP * GRID_W
    chunks = n_seq // chunk
    qspec = pl.BlockSpec((chunk, LANES), lambda b, hp, c: (b * chunks + c, hp))
    kvspec = pl.BlockSpec((n_seq, LANES), lambda b, hp, c: (b, hp))
    bspec = pl.BlockSpec((NA_MAX_KH, 1, NA_MAX_KH * GRID_W, 2 * GRID_W), lambda b, hp, c: (0, hp, 0, 0))
    return pl.pallas_call(
        functools.partial(_na_kernel, rows=rows),
        grid=(batch, NA_HEADS // 2, chunks),
        in_specs=[qspec, kvspec, kvspec, bspec],
        out_specs=qspec,
        out_shape=jax.ShapeDtypeStruct(q.shape, BF16),
        compiler_params=_params("parallel", "parallel", "arbitrary"),
        name="na_attn",
    )(q, k, v, bias)


def _skewed(items, first, second):
    pending = first(items[0])
    for i, item in enumerate(items):
        upcoming = first(items[i + 1]) if i + 1 < len(items) else None
        second(item, pending)
        pending = upcoming


def _nt_dot(a, b):
    return lax.dot_general(a, b, (((1,), (1,)), ((), ())), preferred_element_type=F32)


def _sum8(x):
    return jnp.sum(x.reshape(-1, 8, x.shape[-1]), axis=0)


def _streaming_softmax_matmul(n_chunks, heads, scores, values_t, m_sc, d_sc, acc_sc, write_out,
                              l_sc=None):
    acc_sc[...] = jnp.zeros_like(acc_sc)
    d_sc[...] = jnp.full_like(d_sc, -jnp.inf)
    if l_sc is not None:
        l_sc[...] = jnp.zeros_like(l_sc)

    def step(it, carry):
        def first(item):
            c, h = item
            return scores(it * CHUNKS_PER_ITER + c, h)

        def second(item, st):
            c, h = item
            d_sc[h] = jnp.maximum(d_sc[h], jnp.max(st.reshape(-1, 8, st.shape[-1]), axis=0))
            p = jnp.exp2(st - m_sc[h])
            if l_sc is not None:
                l_sc[h] += _sum8(p)
            acc_sc[h] += jnp.dot(values_t(it * CHUNKS_PER_ITER + c, h), p.astype(BF16),
                                 preferred_element_type=F32)

        _skewed([(c, h) for c in range(CHUNKS_PER_ITER) for h in range(heads)], first, second)
        return carry

    assert n_chunks % CHUNKS_PER_ITER == 0
    lax.fori_loop(0, n_chunks // CHUNKS_PER_ITER, step, 0)
    write_out()

    excess = None
    for h in range(heads):
        e = jnp.max(d_sc[h], axis=0, keepdims=True) - m_sc[h]
        excess = e if excess is None else jnp.maximum(excess, e)

    @pl.when(jnp.max(excess) > SAFE_EXCESS)
    def _():
        acc_sc[...] = jnp.zeros_like(acc_sc)
        m_sc[...] = jnp.full_like(m_sc, -jnp.inf)
        if l_sc is not None:
            l_sc[...] = jnp.zeros_like(l_sc)

        def chunk(c, carry):
            for h in range(heads):
                st = scores(c, h)
                m_prev = m_sc[h]
                m_new = jnp.maximum(m_prev, jnp.max(st, axis=0, keepdims=True))
                alpha = jnp.exp2(m_prev - m_new)
                p = jnp.exp2(st - m_new)
                if l_sc is not None:
                    l_sc[h] = alpha * l_sc[h] + _sum8(p)
                acc_sc[h] = alpha * acc_sc[h] + jnp.dot(values_t(c, h), p.astype(BF16),
                                                        preferred_element_type=F32)
                m_sc[h] = m_new
            return carry

        lax.fori_loop(0, n_chunks, chunk, 0)
        write_out()


def _gqa_kernel(q_ref, k_ref, vt_ref, o_ref, qm_sc, m_sc, d_sc, acc_sc):
    slot = (lax.broadcasted_iota(jnp.int32, (1, 2 * LANES), 1) % LANES) // 32

    def cols(h):
        return slice((h // 4) * 2 * LANES, (h // 4 + 1) * 2 * LANES)

    for h in range(GQA_Q_HEADS):
        qg = q_ref[:, cols(h)]
        qm_sc[h] = jnp.where(slot == h % 4, qg, jnp.zeros_like(qg))
        m_sc[h] = jnp.max(_nt_dot(k_ref[0:PROBE_KEYS, cols(h)], qm_sc[h]), axis=0, keepdims=True)

    def scores(c, h):
        t0 = pl.multiple_of(c * TOKEN_TILE, TOKEN_TILE)
        return _nt_dot(k_ref[pl.ds(t0, TOKEN_TILE), cols(h)], qm_sc[h])

    def values_t(c, h):
        return vt_ref[c, (h // 4) * LANES:(h // 4 + 1) * LANES, :]

    def write_out():
        for hp in range(GQA_Q_HEADS // 2):
            halves = []
            for h in (2 * hp, 2 * hp + 1):
                acc = acc_sc[h]
                halves.append(acc[0:HEAD_DIM] * (1.0 / acc[HEAD_DIM:HEAD_DIM + 1]))
            o_ref[:, hp * LANES:(hp + 1) * LANES] = jnp.concatenate(halves, axis=0).T.astype(BF16)

    _streaming_softmax_matmul(vt_ref.shape[0], GQA_Q_HEADS, scores, values_t, m_sc, d_sc, acc_sc,
                              write_out)


def _gqa_attention(q, k, vt, batch, n_seq):
    tq = GQA_Q_TILE
    assert n_seq % tq == 0 and n_seq % TOKEN_TILE == 0
    nq, n_chunks = n_seq // tq, n_seq // TOKEN_TILE
    return pl.pallas_call(
        _gqa_kernel,
        grid=(batch, nq),
        in_specs=[pl.BlockSpec((tq, BRANCH_WIDTH), lambda b, i: (b * nq + i, 0)),
                  pl.BlockSpec((n_seq, BRANCH_WIDTH), lambda b, i: (b, 0)),
                  pl.BlockSpec((n_chunks, 4 * HEAD_DIM, TOKEN_TILE), lambda b, i: (b, 0, 0))],
        out_specs=pl.BlockSpec((tq, BRANCH_WIDTH), lambda b, i: (b * nq + i, 0)),
        out_shape=jax.ShapeDtypeStruct((batch * n_seq, BRANCH_WIDTH), BF16),
        scratch_shapes=[pltpu.VMEM((GQA_Q_HEADS, tq, 2 * LANES), BF16),
                        pltpu.VMEM((GQA_Q_HEADS, 1, tq), F32),
                        pltpu.VMEM((GQA_Q_HEADS, 8, tq), F32),
                        pltpu.VMEM((GQA_Q_HEADS, LANES, tq), F32)],
        compiler_params=_params("parallel", "arbitrary"),
        name="gqa_attn",
    )(q, k, vt)


def _diff_kernel(q_ref, k_ref, vt_ref, lam_ref, g_ref, o_ref, qm_sc, m_sc, d_sc, l_sc, acc_sc, *,
                 lam_init):
    slot = (lax.broadcasted_iota(jnp.int32, (1, 2 * LANES), 1) % LANES) // 32
    subheads = 2 * DIFF_HEADS

    def cols(j):
        return slice((j // 4) * 2 * LANES, (j // 4 + 1) * 2 * LANES)

    for j in range(subheads):
        qg = q_ref[:, cols(j)]
        qm_sc[j] = jnp.where(slot == j % 4, qg, jnp.zeros_like(qg))
        m_sc[j] = jnp.max(_nt_dot(k_ref[0:PROBE_KEYS, cols(j)], qm_sc[j]), axis=0, keepdims=True)

    def scores(c, j):
        t0 = pl.multiple_of(c * TOKEN_TILE, TOKEN_TILE)
        return _nt_dot(k_ref[pl.ds(t0, TOKEN_TILE), cols(j)], qm_sc[j])

    def values_t(c, j):
        return vt_ref[c, (j // 2) * DIFF_VDIM:(j // 2 + 1) * DIFF_VDIM, :]

    lp = lam_ref[...]
    lam = (jnp.exp(jnp.sum(lp[0:1] * lp[1:2], axis=-1, keepdims=True))
           - jnp.exp(jnp.sum(lp[2:3] * lp[3:4], axis=-1, keepdims=True)) + lam_init)

    def write_out():
        for h in range(DIFF_HEADS):
            l1 = jnp.sum(l_sc[2 * h], axis=0, keepdims=True)
            l2 = jnp.sum(l_sc[2 * h + 1], axis=0, keepdims=True)
            ot = acc_sc[2 * h] * (1.0 / l1) - acc_sc[2 * h + 1] * (lam / l2)
            o = _rms(ot.T) * g_ref[...] * (1.0 - lam_init)
            o_ref[:, h * DIFF_VDIM:(h + 1) * DIFF_VDIM] = o.astype(BF16)

    _streaming_softmax_matmul(vt_ref.shape[0], subheads, scores, values_t, m_sc, d_sc, acc_sc,
                              write_out, l_sc=l_sc)


def _diff_attention(q, k, vt, lam_params, subln_g, lam_init, batch, n_seq):
    tq = DIFF_Q_TILE
    assert n_seq % tq == 0 and n_seq % TOKEN_TILE == 0
    nq, n_chunks = n_seq // tq, n_seq // TOKEN_TILE
    subheads = 2 * DIFF_HEADS
    return pl.pallas_call(
        functools.partial(_diff_kernel, lam_init=lam_init),
        grid=(batch, nq),
        in_specs=[pl.BlockSpec((tq, BRANCH_WIDTH), lambda b, i: (b * nq + i, 0)),
                  pl.BlockSpec((n_seq, BRANCH_WIDTH), lambda b, i: (b, 0)),
                  pl.BlockSpec((n_chunks, BRANCH_WIDTH, TOKEN_TILE), lambda b, i: (b, 0, 0)),
                  _const_spec((4, HEAD_DIM)), _const_spec((1, DIFF_VDIM))],
        out_specs=pl.BlockSpec((tq, BRANCH_WIDTH), lambda b, i: (b * nq + i, 0)),
        out_shape=jax.ShapeDtypeStruct((batch * n_seq, BRANCH_WIDTH), BF16),
        scratch_shapes=[pltpu.VMEM((subheads, tq, 2 * LANES), BF16),
                        pltpu.VMEM((subheads, 1, tq), F32),
                        pltpu.VMEM((subheads, 8, tq), F32),
                        pltpu.VMEM((subheads, 8, tq), F32),
                        pltpu.VMEM((subheads, DIFF_VDIM, tq), F32)],
        compiler_params=_params("parallel", "arbitrary"),
        name="diff_attn",
    )(q, k, vt, lam_params, subln_g)


def _merge_kernel(x_ref, g_ref, oa_ref, ob_ref, oc_ref, wg_ref, wb_ref, wo_ref, y_ref):
    x = x_ref[...]
    h = (_rms(x) * g_ref[...]).astype(BF16)
    merged = None
    for i, o_ref in enumerate((oa_ref, ob_ref, oc_ref)):
        gate = jax.nn.sigmoid(jnp.dot(h, wg_ref[:, i * D_MODEL:(i + 1) * D_MODEL],
                                      preferred_element_type=F32))
        term = gate * jnp.dot(o_ref[...], wb_ref[i], preferred_element_type=F32)
        merged = term if merged is None else merged + term
    y_ref[...] = x + jnp.dot(merged.astype(BF16), wo_ref[...], preferred_element_type=F32)


def _merge(x2d, gain, oa, ob, oc, w_gate, w_branch, w_out):
    t = x2d.shape[0]
    tm = TOKEN_TILE
    tok = lambda c: pl.BlockSpec((tm, c), lambda i: (i, 0))
    return pl.pallas_call(
        _merge_kernel,
        grid=(t // tm,),
        in_specs=[tok(D_MODEL), _const_spec((1, D_MODEL)), tok(BRANCH_WIDTH), tok(BRANCH_WIDTH),
                  tok(BRANCH_WIDTH), _const_spec((D_MODEL, 3 * D_MODEL)),
                  _const_spec((3, BRANCH_WIDTH, D_MODEL)), _const_spec((D_MODEL, D_MODEL))],
        out_specs=tok(D_MODEL),
        out_shape=jax.ShapeDtypeStruct((t, D_MODEL), F32),
        compiler_params=_params("parallel"),
        name="merge",
    )(x2d, gain, oa, ob, oc, w_gate, w_branch, w_out)


FFN_CHUNKS = 2


def _ffn_kernel(x_ref, g_ref, wg_ref, wu_ref, wd_ref, gf_ref, y_ref, *, final_norm):
    x = x_ref[...]
    h = (_rms(x) * g_ref[...]).astype(BF16)
    width = D_FF // FFN_CHUNKS
    y = x
    for c in range(FFN_CHUNKS):
        cs = slice(c * width, (c + 1) * width)
        a = jnp.dot(h, wg_ref[:, cs], preferred_element_type=F32)
        b = jnp.dot(h, wu_ref[:, cs], preferred_element_type=F32)
        act = (a * jax.nn.sigmoid(a) * b).astype(BF16)
        y = y + jnp.dot(act, wd_ref[cs, :], preferred_element_type=F32)
    if final_norm:
        y = _rms(y) * gf_ref[...]
    y_ref[...] = y


def _ffn(x2d, gain, w_gate, w_up, w_down, gain_final, final_norm):
    t = x2d.shape[0]
    tm = TOKEN_TILE
    tok = pl.BlockSpec((tm, D_MODEL), lambda i: (i, 0))
    return pl.pallas_call(
        functools.partial(_ffn_kernel, final_norm=final_norm),
        grid=(t // tm,),
        in_specs=[tok, _const_spec((1, D_MODEL)), _const_spec((D_MODEL, D_FF)),
                  _const_spec((D_MODEL, D_FF)), _const_spec((D_FF, D_MODEL)),
                  _const_spec((1, D_MODEL))],
        out_specs=tok,
        out_shape=jax.ShapeDtypeStruct((t, D_MODEL), F32),
        compiler_params=_params("parallel"),
        name="ffn",
    )(x2d, gain, w_gate, w_up, w_down, gain_final)


def _prepare_layer(l, norm_mix, w_in, na_rpb, qk_norm, diff_lambda, diff_subln, w_branch, w_out,
                   norm_ffn, w_ffn_gate, w_ffn_up, w_ffn_down):
    row = lambda g: g.astype(F32).reshape(1, -1)
    return dict(
        lam_init=0.8 - 0.6 * math.exp(-0.3 * l),
        norm_mix=row(norm_mix[l]),
        w_qkv=w_in[l][:, _COL_PERM].astype(BF16),
        w_gate=w_in[l][:, QKV_COLS:].astype(BF16),
        na_bias=_na_bias_table(na_rpb[l]),
        gq=_gqa_gain_tiles(qk_norm[l, 0]),
        gk=_gqa_gain_tiles(qk_norm[l, 1]),
        diff_lambda=diff_lambda[l].astype(F32),
        diff_subln=row(diff_subln[l]),
        w_branch=w_branch[l].astype(BF16),
        w_out=w_out[l].astype(BF16),
        norm_ffn=row(norm_ffn[l]),
        w_ffn_gate=w_ffn_gate[l].astype(BF16),
        w_ffn_up=w_ffn_up[l].astype(BF16),
        w_ffn_down=w_ffn_down[l].astype(BF16),
    )


def _trunk(x, layers, norm_final):
    batch, n_seq, _ = x.shape
    x2d = x.reshape(batch * n_seq, D_MODEL)
    tables = _rope_tables(n_seq)
    gf = norm_final.astype(F32).reshape(1, -1)
    for l, p in enumerate(layers):
        qa, ka, va, qb, kb, vb, qc, kc, vc = _inproj(x2d, n_seq, p["norm_mix"], p["w_qkv"], tables,
                                                     p["gq"], p["gk"])
        o_a = _na_attention(qa, ka, va, p["na_bias"], batch, n_seq)
        o_b = _gqa_attention(qb, kb, vb, batch, n_seq)
        o_c = _diff_attention(qc, kc, vc, p["diff_lambda"], p["diff_subln"], p["lam_init"], batch, n_seq)
        x2d = _merge(x2d, p["norm_mix"], o_a, o_b, o_c, p["w_gate"], p["w_branch"], p["w_out"])
        x2d = _ffn(x2d, p["norm_ffn"], p["w_ffn_gate"], p["w_ffn_up"], p["w_ffn_down"], gf,
                   final_norm=(l == len(layers) - 1))
    return x2d.reshape(batch, n_seq, D_MODEL)


def kernel(x_prompt, x_sample, norm_mix, w_in, na_rpb, qk_norm, diff_lambda, diff_subln, w_branch,
           w_out, norm_ffn, w_ffn_gate, w_ffn_up, w_ffn_down, norm_final):
    depth = w_in.shape[0]
    layers = [_prepare_layer(l, norm_mix, w_in, na_rpb, qk_norm, diff_lambda, diff_subln, w_branch,
                             w_out, norm_ffn, w_ffn_gate, w_ffn_up, w_ffn_down) for l in range(depth)]
    return (_trunk(x_prompt, layers, norm_final), _trunk(x_sample, layers, norm_final))
```

```python
import functools
import math

import jax
import jax.numpy as jnp
import numpy as np
from jax import lax
from jax.experimental import pallas as pl
from jax.experimental.pallas import tpu as pltpu

F32 = jnp.float32
BF16 = jnp.bfloat16

D_MODEL = 1024
GRID_W = 64
HEAD_DIM = 64
NA_HEADS = 8
NA_MAX_KH = 8
NA_KW = 16
GQA_Q_HEADS = 8
GQA_KV_HEADS = 2
DIFF_HEADS = 4
DIFF_VDIM = 128
BRANCH_WIDTH = 512
ROPE_THETA = 10000.0
RMS_EPS = 1e-6
D_FF = 2816
BF16_SUBLANES = 16
GQA_VT_ROWS = HEAD_DIM + BF16_SUBLANES
DIFF_VT_ROWS = DIFF_VDIM + BF16_SUBLANES
QKV_COLS = 3840

LANES = 128
VMEM_LIMIT_BYTES = 56 * 1024 * 1024

LOG2E = 1.4426950408889634
Q_SCALE = HEAD_DIM ** -0.5 * LOG2E
NEG_BIG = -1e30

TOKEN_TILE = 512
NA_ROWS_PER_STEP = 8
GQA_Q_TILE = 512
DIFF_Q_TILE = 512
CHUNKS_PER_ITER = 4
SAFE_EXCESS = 64.0
PROBE_KEYS = 128


def _const_spec(shape):
    nd = len(shape)
    return pl.BlockSpec(shape, lambda *_: (0,) * nd, pipeline_mode=pl.Buffered(1))


def _params(*sem):
    return pltpu.CompilerParams(dimension_semantics=sem, vmem_limit_bytes=VMEM_LIMIT_BYTES)


def _rms(x, eps=RMS_EPS):
    return x * lax.rsqrt(jnp.mean(x * x, axis=-1, keepdims=True) + eps)


def _gqa_lane(qd, i, slot):
    return (qd % 2) * 128 + slot * 32 + (qd // 2) * 16 + i


def _build_column_perm():
    cols = []
    cols += list(range(0, 1536))
    qb = np.zeros(512, np.int64)
    for h in range(GQA_Q_HEADS):
        for d in range(HEAD_DIM):
            qb[(h // 4) * 256 + _gqa_lane(d // 16, d % 16, h % 4)] = 1536 + h * 64 + d
    cols += list(qb)
    kb = np.zeros(512, np.int64)
    for v in range(GQA_KV_HEADS):
        for d in range(HEAD_DIM):
            for slot in range(4):
                kb[v * 256 + _gqa_lane(d // 16, d % 16, slot)] = 2048 + v * 64 + d
    cols += list(kb)
    cols += list(range(2176, 2304))
    for base in (2304, 2816):
        p = np.zeros(512, np.int64)
        for j in range(2 * DIFF_HEADS):
            for d in range(HEAD_DIM):
                p[(j // 4) * 256 + (d // 32) * 128 + (j % 4) * 32 + d % 32] = base + j * 64 + d
        cols += list(p)
    cols += list(range(3328, 3840))
    return np.asarray(cols, np.int32)


_COL_PERM = _build_column_perm()
_C_QA, _C_KA, _C_VA, _C_QB, _C_KB, _C_VB, _C_QC, _C_KC, _C_VC, _C_END = (
    0, 512, 1024, 1536, 2048, 2560, 2688, 3200, 3712, 4224)


def _gqa_gain_tiles(g):
    idx = np.zeros((2, 128), np.int32)
    for slot in range(4):
        for ab in range(2):
            for i in range(16):
                idx[0, slot * 32 + ab * 16 + i] = ab * 32 + i
                idx[1, slot * 32 + ab * 16 + i] = ab * 32 + 16 + i
    return g.astype(F32)[idx]


def _rope_tables(n):
    t = jnp.arange(n)
    inv32 = 1.0 / (ROPE_THETA ** (jnp.arange(0, 32, 2, dtype=F32) / 32))
    inv64 = 1.0 / (ROPE_THETA ** (jnp.arange(0, 64, 2, dtype=F32) / 64))
    ang_r = (t // GRID_W).astype(F32)[:, None] * inv32[None, :]
    ang_c = (t % GRID_W).astype(F32)[:, None] * inv32[None, :]
    ang_g = jnp.tile(jnp.concatenate([ang_r, ang_c], axis=-1), (1, 4))
    ang_d = jnp.tile(t.astype(F32)[:, None] * inv64[None, :], (1, 4))
    return jnp.cos(ang_g), jnp.sin(ang_g), jnp.cos(ang_d), jnp.sin(ang_d)


def _inproj_kernel(x_ref, g_ref, w_ref, cg_ref, sg_ref, cd_ref, sd_ref, gq_ref, gk_ref,
                   qa_ref, ka_ref, va_ref, qb_ref, kb_ref, vb_ref, qc_ref, kc_ref, vc_ref):
    h = (_rms(x_ref[...]) * g_ref[...]).astype(BF16)

    def proj(c0, c1):
        return jnp.dot(h, w_ref[:, c0:c1], preferred_element_type=F32)

    qa_ref[...] = (proj(_C_QA, _C_KA) * Q_SCALE).astype(BF16)
    ka_ref[...] = proj(_C_KA, _C_VA).astype(BF16)
    va_ref[...] = proj(_C_VA, _C_QB).astype(BF16)
    vt = proj(_C_VB, _C_QC).T
    ones = jnp.ones((GQA_VT_ROWS - HEAD_DIM, vt.shape[1]), F32)
    vb_ref[0] = jnp.concatenate([vt[0:HEAD_DIM], ones, vt[HEAD_DIM:], ones], axis=0).astype(BF16)
    vct = proj(_C_VC, _C_END).T
    ones = jnp.ones((DIFF_VT_ROWS - DIFF_VDIM, vct.shape[1]), F32)
    vc_ref[0] = jnp.concatenate(
        [piece for h in range(DIFF_HEADS) for piece in (vct[h * DIFF_VDIM:(h + 1) * DIFF_VDIM], ones)],
        axis=0).astype(BF16)

    slot = lax.broadcasted_iota(jnp.int32, (1, LANES), 1) // 32

    def rope(x1, x2, c, s):
        return x1 * c - x2 * s, x2 * c + x1 * s

    cg, sg = cg_ref[...], sg_ref[...]

    def gqa(y, gains_ref, out_ref, per_slot, scale):
        for grp in range(2):
            p = y[:, grp * 256:grp * 256 + 128]
            q = y[:, grp * 256 + 128:grp * 256 + 256]
            ssq = p * p + q * q
            if per_slot:
                r = jnp.zeros_like(ssq)
                for sl in range(4):
                    tot = jnp.sum(jnp.where(slot == sl, ssq, 0.0), axis=-1, keepdims=True)
                    r = jnp.where(slot == sl, lax.rsqrt(tot * (1.0 / HEAD_DIM) + RMS_EPS), r)
            else:
                tot = jnp.sum(jnp.where(slot == 0, ssq, 0.0), axis=-1, keepdims=True)
                r = lax.rsqrt(tot * (1.0 / HEAD_DIM) + RMS_EPS)
            o1, o2 = rope(p * r * gains_ref[0:1, :], q * r * gains_ref[1:2, :], cg, sg)
            out_ref[:, grp * 256:grp * 256 + 128] = (o1 * scale).astype(BF16)
            out_ref[:, grp * 256 + 128:grp * 256 + 256] = (o2 * scale).astype(BF16)

    gqa(proj(_C_QB, _C_KB), gq_ref, qb_ref, True, Q_SCALE)
    gqa(proj(_C_KB, _C_VB), gk_ref, kb_ref, False, 1.0)

    cd, sd = cd_ref[...], sd_ref[...]

    def diff(y, out_ref, scale):
        for grp in range(2):
            o1, o2 = rope(y[:, grp * 256:grp * 256 + 128], y[:, grp * 256 + 128:grp * 256 + 256], cd, sd)
            out_ref[:, grp * 256:grp * 256 + 128] = (o1 * scale).astype(BF16)
            out_ref[:, grp * 256 + 128:grp * 256 + 256] = (o2 * scale).astype(BF16)

    diff(proj(_C_QC, _C_KC), qc_ref, Q_SCALE)
    diff(proj(_C_KC, _C_VC), kc_ref, 1.0)


def _inproj(x2d, n_seq, gain, w_perm, tables, gq, gk):
    t = x2d.shape[0]
    tm = TOKEN_TILE
    assert t % tm == 0 and n_seq % tm == 0
    seq_tiles = n_seq // tm
    tok = lambda c: pl.BlockSpec((tm, c), lambda i: (i, 0))
    tab = pl.BlockSpec((tm, LANES), lambda i: (i % seq_tiles, 0))
    widths = (512, 512, 512, 512, 512, -GQA_KV_HEADS * GQA_VT_ROWS, 512, 512,
              -DIFF_HEADS * DIFF_VT_ROWS)
    spec = lambda c: tok(c) if c > 0 else pl.BlockSpec((1, -c, tm), lambda i: (i, 0, 0))
    shape = lambda c: jax.ShapeDtypeStruct((t, c) if c > 0 else (t // tm, -c, tm), BF16)
    return pl.pallas_call(
        _inproj_kernel,
        grid=(t // tm,),
        in_specs=[tok(D_MODEL), _const_spec((1, D_MODEL)), _const_spec((D_MODEL, _C_END)),
                  tab, tab, tab, tab, _const_spec((2, LANES)), _const_spec((2, LANES))],
        out_specs=[spec(c) for c in widths],
        out_shape=[shape(c) for c in widths],
        compiler_params=_params("parallel"),
        name="inproj",
    )(x2d, gain, w_perm, *tables, gq, gk)


def _na_bias_table(rpb):
    qc = np.arange(GRID_W)[:, None]
    kc = np.arange(GRID_W)[None, :]
    w_start = np.clip(qc - NA_KW // 2, 0, GRID_W - NA_KW)
    valid = (kc >= w_start) & (kc < w_start + NA_KW)
    dc = np.clip(kc - qc, -(NA_KW - 1), NA_KW - 1) + NA_KW - 1
    off = np.arange(NA_MAX_KH)[:, None]
    win = np.arange(NA_MAX_KH)[None, :]
    dr = win + (NA_MAX_KH - 1) - off
    b = rpb.astype(F32)[:, dr][:, :, :, dc]
    b = jnp.where(valid[None, None, None], b * LOG2E, NEG_BIG)
    b = b.reshape(NA_HEADS // 2, 2, NA_MAX_KH, NA_MAX_KH, GRID_W, GRID_W)
    b = b.transpose(2, 0, 3, 5, 1, 4)
    return b.reshape(NA_MAX_KH, NA_HEADS // 2, NA_MAX_KH * GRID_W, 2 * GRID_W)


def _na_kernel(q_ref, k_ref, v_ref, b_ref, o_ref, *, rows):
    c = pl.program_id(2)
    lo = lax.broadcasted_iota(jnp.int32, (1, LANES), 1) < HEAD_DIM
    win_tokens = NA_MAX_KH * GRID_W

    def window(i):
        r = c * NA_ROWS_PER_STEP + i
        r_start = jnp.clip(r - NA_MAX_KH // 2, 0, rows - NA_MAX_KH)
        return r - r_start, pl.multiple_of(r_start * GRID_W, GRID_W)

    def scores(i):
        _, t0 = window(i)
        q = q_ref[i * GRID_W:(i + 1) * GRID_W, :]
        zero = jnp.zeros_like(q)
        qq = jnp.concatenate([jnp.where(lo, q, zero), jnp.where(lo, zero, q)], axis=0)
        return _nt_dot(k_ref[pl.ds(t0, win_tokens), :], qq)

    def attend(i, st):
        off, t0 = window(i)
        st = st + b_ref[off, 0]
        p = jnp.exp2(st - jnp.max(st, axis=0, keepdims=True))
        p = p * (1.0 / jnp.sum(p, axis=0, keepdims=True))
        o2 = jnp.dot(p.T.astype(BF16), v_ref[pl.ds(t0, win_tokens), :],
                     preferred_element_type=F32)
        o_ref[i * GRID_W:(i + 1) * GRID_W, :] = jnp.where(lo, o2[0:GRID_W], o2[GRID_W:]).astype(BF16)

    _skewed(list(range(NA_ROWS_PER_STEP)), scores, attend)


def _na_attention(q, k, v, bias, batch, n_seq):
    rows = n_seq // GRID_W
    assert rows >= NA_MAX_KH and rows % NA_ROWS_PER_STEP == 0
    chunk = NA_ROWS_PER_STEP * GRID_W
    chunks = n_seq // chunk
    qspec = pl.BlockSpec((chunk, LANES), lambda b, hp, c: (b * chunks + c, hp))
    kvspec = pl.BlockSpec((n_seq, LANES), lambda b, hp, c: (b, hp))
    bspec = pl.BlockSpec((NA_MAX_KH, 1, NA_MAX_KH * GRID_W, 2 * GRID_W), lambda b, hp, c: (0, hp, 0, 0))
    return pl.pallas_call(
        functools.partial(_na_kernel, rows=rows),
        grid=(batch, NA_HEADS // 2, chunks),
        in_specs=[qspec, kvspec, kvspec, bspec],
        out_specs=qspec,
        out_shape=jax.ShapeDtypeStruct(q.shape, BF16),
        compiler_params=_params("parallel", "parallel", "arbitrary"),
        name="na_attn",
    )(q, k, v, bias)


def _skewed(items, first, second):
    pending = first(items[0])
    for i, item in enumerate(items):
        upcoming = first(items[i + 1]) if i + 1 < len(items) else None
        second(item, pending)
        pending = upcoming


def _nt_dot(a, b):
    return lax.dot_general(a, b, (((1,), (1,)), ((), ())), preferred_element_type=F32)


def _exp2_bf16(x):
    return jnp.exp2(x.astype(BF16))


def _streaming_softmax_matmul(n_chunks, heads, scores, values_t, m_sc, d_sc, acc_sc, write_out):
    acc_sc[...] = jnp.zeros_like(acc_sc)
    d_sc[...] = jnp.full_like(d_sc, -jnp.inf)
    per_iter = math.gcd(n_chunks, CHUNKS_PER_ITER)

    def step(it, carry):
        def first(item):
            c, h = item
            return scores(it * per_iter + c, h)

        def second(item, st):
            c, h = item
            d_sc[h] = jnp.maximum(d_sc[h], jnp.max(st.reshape(-1, 8, st.shape[-1]), axis=0))
            acc_sc[h] += jnp.dot(values_t(it * per_iter + c, h), _exp2_bf16(st - m_sc[h]),
                                 preferred_element_type=F32)

        _skewed([(c, h) for c in range(per_iter) for h in range(heads)], first, second)
        return carry

    lax.fori_loop(0, n_chunks // per_iter, step, 0)
    write_out()

    excess = None
    for h in range(heads):
        e = jnp.max(d_sc[h], axis=0, keepdims=True) - m_sc[h]
        excess = e if excess is None else jnp.maximum(excess, e)

    @pl.when(jnp.max(excess) > SAFE_EXCESS)
    def _():
        acc_sc[...] = jnp.zeros_like(acc_sc)
        m_sc[...] = jnp.full_like(m_sc, -jnp.inf)

        def chunk(c, carry):
            for h in range(heads):
                st = scores(c, h)
                m_prev = m_sc[h]
                m_new = jnp.maximum(m_prev, jnp.max(st, axis=0, keepdims=True))
                acc_sc[h] = (jnp.exp2(m_prev - m_new) * acc_sc[h]
                             + jnp.dot(values_t(c, h), _exp2_bf16(st - m_new),
                                       preferred_element_type=F32))
                m_sc[h] = m_new
            return carry

        lax.fori_loop(0, n_chunks, chunk, 0)
        write_out()


def _group_cols(h):
    return slice((h // 4) * 2 * LANES, (h // 4 + 1) * 2 * LANES)


def _prepare_queries(q_ref, k_ref, qm_sc, m_sc):
    qt = q_ref[...].astype(F32).T
    slot = (lax.broadcasted_iota(jnp.int32, (2 * LANES, 1), 0) % LANES) // 32
    for h in range(qm_sc.shape[0]):
        qg = qt[_group_cols(h)]
        qm_sc[h] = jnp.where(slot == h % 4, qg, 0.0).astype(BF16)
        probe = jnp.dot(k_ref[0:PROBE_KEYS, _group_cols(h)], qm_sc[h], preferred_element_type=F32)
        m_sc[h] = jnp.max(probe, axis=0, keepdims=True)


def _chunk_scores(k_ref, qm_sc, c, h):
    t0 = pl.multiple_of(c * TOKEN_TILE, TOKEN_TILE)
    return jnp.dot(k_ref[pl.ds(t0, TOKEN_TILE), _group_cols(h)], qm_sc[h],
                   preferred_element_type=F32)


def _gqa_kernel(q_ref, k_ref, vt_ref, o_ref, qm_sc, m_sc, d_sc, acc_sc):
    _prepare_queries(q_ref, k_ref, qm_sc, m_sc)
    scores = functools.partial(_chunk_scores, k_ref, qm_sc)

    def values_t(c, h):
        return vt_ref[c, (h // 4) * GQA_VT_ROWS:(h // 4 + 1) * GQA_VT_ROWS, :]

    def write_out():
        for hp in range(GQA_Q_HEADS // 2):
            halves = []
            for h in (2 * hp, 2 * hp + 1):
                acc = acc_sc[h]
                halves.append(acc[0:HEAD_DIM] * (1.0 / acc[HEAD_DIM:HEAD_DIM + 1]))
            o_ref[:, hp * LANES:(hp + 1) * LANES] = jnp.concatenate(halves, axis=0).T.astype(BF16)

    _streaming_softmax_matmul(vt_ref.shape[0], GQA_Q_HEADS, scores, values_t, m_sc, d_sc, acc_sc,
                              write_out)


def _gqa_attention(q, k, vt, batch, n_seq):
    tq = GQA_Q_TILE
    assert n_seq % tq == 0 and n_seq % TOKEN_TILE == 0
    nq, n_chunks = n_seq // tq, n_seq // TOKEN_TILE
    return pl.pallas_call(
        _gqa_kernel,
        grid=(batch, nq),
        in_specs=[pl.BlockSpec((tq, BRANCH_WIDTH), lambda b, i: (b * nq + i, 0)),
                  pl.BlockSpec((n_seq, BRANCH_WIDTH), lambda b, i: (b, 0)),
                  pl.BlockSpec((n_chunks, GQA_KV_HEADS * GQA_VT_ROWS, TOKEN_TILE),
                               lambda b, i: (b, 0, 0))],
        out_specs=pl.BlockSpec((tq, BRANCH_WIDTH), lambda b, i: (b * nq + i, 0)),
        out_shape=jax.ShapeDtypeStruct((batch * n_seq, BRANCH_WIDTH), BF16),
        scratch_shapes=[pltpu.VMEM((GQA_Q_HEADS, 2 * LANES, tq), BF16),
                        pltpu.VMEM((GQA_Q_HEADS, 1, tq), F32),
                        pltpu.VMEM((GQA_Q_HEADS, 8, tq), F32),
                        pltpu.VMEM((GQA_Q_HEADS, GQA_VT_ROWS, tq), F32)],
        compiler_params=_params("parallel", "arbitrary"),
        name="gqa_attn",
    )(q, k, vt)


def _diff_kernel(q_ref, k_ref, vt_ref, lam_ref, g_ref, o_ref, qm_sc, m_sc, d_sc, acc_sc, *, lam_init):
    subheads = 2 * DIFF_HEADS
    _prepare_queries(q_ref, k_ref, qm_sc, m_sc)
    scores = functools.partial(_chunk_scores, k_ref, qm_sc)

    def values_t(c, j):
        return vt_ref[c, (j // 2) * DIFF_VT_ROWS:(j // 2 + 1) * DIFF_VT_ROWS, :]

    lp = lam_ref[...]
    lam = (jnp.exp(jnp.sum(lp[0:1] * lp[1:2], axis=-1, keepdims=True))
           - jnp.exp(jnp.sum(lp[2:3] * lp[3:4], axis=-1, keepdims=True)) + lam_init)

    def write_out():
        for h in range(DIFF_HEADS):
            a1, a2 = acc_sc[2 * h], acc_sc[2 * h + 1]
            ot = (a1[0:DIFF_VDIM] * (1.0 / a1[DIFF_VDIM:DIFF_VDIM + 1])
                  - a2[0:DIFF_VDIM] * (lam / a2[DIFF_VDIM:DIFF_VDIM + 1]))
            o = _rms(ot.T) * g_ref[...] * (1.0 - lam_init)
            o_ref[:, h * DIFF_VDIM:(h + 1) * DIFF_VDIM] = o.astype(BF16)

    _streaming_softmax_matmul(vt_ref.shape[0], subheads, scores, values_t, m_sc, d_sc, acc_sc,
                              write_out)


def _diff_attention(q, k, vt, lam_params, subln_g, lam_init, batch, n_seq):
    tq = DIFF_Q_TILE
    assert n_seq % tq == 0 and n_seq % TOKEN_TILE == 0
    nq, n_chunks = n_seq // tq, n_seq // TOKEN_TILE
    subheads = 2 * DIFF_HEADS
    return pl.pallas_call(
        functools.partial(_diff_kernel, lam_init=lam_init),
        grid=(batch, nq),
        in_specs=[pl.BlockSpec((tq, BRANCH_WIDTH), lambda b, i: (b * nq + i, 0)),
                  pl.BlockSpec((n_seq, BRANCH_WIDTH), lambda b, i: (b, 0)),
                  pl.BlockSpec((n_chunks, DIFF_HEADS * DIFF_VT_ROWS, TOKEN_TILE),
                               lambda b, i: (b, 0, 0)),
                  _const_spec((4, HEAD_DIM)), _const_spec((1, DIFF_VDIM))],
        out_specs=pl.BlockSpec((tq, BRANCH_WIDTH), lambda b, i: (b * nq + i, 0)),
        out_shape=jax.ShapeDtypeStruct((batch * n_seq, BRANCH_WIDTH), BF16),
        scratch_shapes=[pltpu.VMEM((subheads, 2 * LANES, tq), BF16),
                        pltpu.VMEM((subheads, 1, tq), F32),
                        pltpu.VMEM((subheads, 8, tq), F32),
                        pltpu.VMEM((subheads, DIFF_VT_ROWS, tq), F32)],
        compiler_params=_params("parallel", "arbitrary"),
        name="diff_attn",
    )(q, k, vt, lam_params, subln_g)


def _merge_kernel(x_ref, g_ref, oa_ref, ob_ref, oc_ref, wg_ref, wb_ref, wo_ref, y_ref):
    x = x_ref[...]
    h = (_rms(x) * g_ref[...]).astype(BF16)
    merged = None
    for i, o_ref in enumerate((oa_ref, ob_ref, oc_ref)):
        gate = jax.nn.sigmoid(jnp.dot(h, wg_ref[:, i * D_MODEL:(i + 1) * D_MODEL],
                                      preferred_element_type=F32))
        term = gate * jnp.dot(o_ref[...], wb_ref[i], preferred_element_type=F32)
        merged = term if merged is None else merged + term
    y_ref[...] = x + jnp.dot(merged.astype(BF16), wo_ref[...], preferred_element_type=F32)


def _merge(x2d, gain, oa, ob, oc, w_gate, w_branch, w_out):
    t = x2d.shape[0]
    tm = TOKEN_TILE
    tok = lambda c: pl.BlockSpec((tm, c), lambda i: (i, 0))
    return pl.pallas_call(
        _merge_kernel,
        grid=(t // tm,),
        in_specs=[tok(D_MODEL), _const_spec((1, D_MODEL)), tok(BRANCH_WIDTH), tok(BRANCH_WIDTH),
                  tok(BRANCH_WIDTH), _const_spec((D_MODEL, 3 * D_MODEL)),
                  _const_spec((3, BRANCH_WIDTH, D_MODEL)), _const_spec((D_MODEL, D_MODEL))],
        out_specs=tok(D_MODEL),
        out_shape=jax.ShapeDtypeStruct((t, D_MODEL), F32),
        compiler_params=_params("parallel"),
        name="merge",
    )(x2d, gain, oa, ob, oc, w_gate, w_branch, w_out)


FFN_CHUNKS = 2


def _ffn_kernel(x_ref, g_ref, wg_ref, wu_ref, wd_ref, gf_ref, y_ref, *, final_norm):
    x = x_ref[...]
    h = (_rms(x) * g_ref[...]).astype(BF16)
    width = D_FF // FFN_CHUNKS
    y = x
    for c in range(FFN_CHUNKS):
        cs = slice(c * width, (c + 1) * width)
        a = jnp.dot(h, wg_ref[:, cs], preferred_element_type=F32)
        b = jnp.dot(h, wu_ref[:, cs], preferred_element_type=F32)
        act = (a * jax.nn.sigmoid(a) * b).astype(BF16)
        y = y + jnp.dot(act, wd_ref[cs, :], preferred_element_type=F32)
    if final_norm:
        y = _rms(y) * gf_ref[...]
    y_ref[...] = y


def _ffn(x2d, gain, w_gate, w_up, w_down, gain_final, final_norm):
    t = x2d.shape[0]
    tm = TOKEN_TILE
    tok = pl.BlockSpec((tm, D_MODEL), lambda i: (i, 0))
    return pl.pallas_call(
        functools.partial(_ffn_kernel, final_norm=final_norm),
        grid=(t // tm,),
        in_specs=[tok, _const_spec((1, D_MODEL)), _const_spec((D_MODEL, D_FF)),
                  _const_spec((D_MODEL, D_FF)), _const_spec((D_FF, D_MODEL)),
                  _const_spec((1, D_MODEL))],
        out_specs=tok,
        out_shape=jax.ShapeDtypeStruct((t, D_MODEL), F32),
        compiler_params=_params("parallel"),
        name="ffn",
    )(x2d, gain, w_gate, w_up, w_down, gain_final)


def _prepare_layer(l, norm_mix, w_in, na_rpb, qk_norm, diff_lambda, diff_subln, w_branch, w_out,
                   norm_ffn, w_ffn_gate, w_ffn_up, w_ffn_down):
    row = lambda g: g.astype(F32).reshape(1, -1)
    return dict(
        lam_init=0.8 - 0.6 * math.exp(-0.3 * l),
        norm_mix=row(norm_mix[l]),
        w_qkv=w_in[l][:, _COL_PERM].astype(BF16),
        w_gate=w_in[l][:, QKV_COLS:].astype(BF16),
        na_bias=_na_bias_table(na_rpb[l]),
        gq=_gqa_gain_tiles(qk_norm[l, 0]),
        gk=_gqa_gain_tiles(qk_norm[l, 1]),
        diff_lambda=diff_lambda[l].astype(F32),
        diff_subln=row(diff_subln[l]),
        w_branch=w_branch[l].astype(BF16),
        w_out=w_out[l].astype(BF16),
        norm_ffn=row(norm_ffn[l]),
        w_ffn_gate=w_ffn_gate[l].astype(BF16),
        w_ffn_up=w_ffn_up[l].astype(BF16),
        w_ffn_down=w_ffn_down[l].astype(BF16),
    )


def _trunk(x, layers, norm_final):
    batch, n_seq, _ = x.shape
    x2d = x.reshape(batch * n_seq, D_MODEL)
    tables = _rope_tables(n_seq)
    gf = norm_final.astype(F32).reshape(1, -1)
    for l, p in enumerate(layers):
        qa, ka, va, qb, kb, vb, qc, kc, vc = _inproj(x2d, n_seq, p["norm_mix"], p["w_qkv"], tables,
                                                     p["gq"], p["gk"])
        o_a = _na_attention(qa, ka, va, p["na_bias"], batch, n_seq)
        o_b = _gqa_attention(qb, kb, vb, batch, n_seq)
        o_c = _diff_attention(qc, kc, vc, p["diff_lambda"], p["diff_subln"], p["lam_init"], batch, n_seq)
        x2d = _merge(x2d, p["norm_mix"], o_a, o_b, o_c, p["w_gate"], p["w_branch"], p["w_out"])
        x2d = _ffn(x2d, p["norm_ffn"], p["w_ffn_gate"], p["w_ffn_up"], p["w_ffn_down"], gf,
                   final_norm=(l == len(layers) - 1))
    return x2d.reshape(batch, n_seq, D_MODEL)


def kernel(x_prompt, x_sample, norm_mix, w_in, na_rpb, qk_norm, diff_lambda, diff_subln, w_branch,
           w_out, norm_ffn, w_ffn_gate, w_ffn_up, w_ffn_down, norm_final):
    depth = w_in.shape[0]
    layers = [_prepare_layer(l, norm_mix, w_in, na_rpb, qk_norm, diff_lambda, diff_subln, w_branch,
                             w_out, norm_ffn, w_ffn_gate, w_ffn_up, w_ffn_down) for l in range(depth)]
    return (_trunk(x_prompt, layers, norm_final), _trunk(x_sample, layers, norm_final))
```

```python
import functools
import math

import jax
import jax.numpy as jnp
import numpy as np
from jax import lax
from jax.experimental import pallas as pl
from jax.experimental.pallas import tpu as pltpu

F32 = jnp.float32
BF16 = jnp.bfloat16

D_MODEL = 1024
GRID_W = 64
HEAD_DIM = 64
NA_HEADS = 8
NA_MAX_KH = 8
NA_KW = 16
NA_BIAS_ROWS = 2 * NA_MAX_KH - 1
GQA_Q_HEADS = 8
GQA_KV_HEADS = 2
DIFF_HEADS = 4
DIFF_VDIM = 128
BRANCH_WIDTH = 512
ROPE_THETA = 10000.0
RMS_EPS = 1e-6
D_FF = 2816
BF16_SUBLANES = 16
GQA_VT_ROWS = HEAD_DIM + BF16_SUBLANES
DIFF_VT_ROWS = DIFF_VDIM + BF16_SUBLANES
QKV_COLS = 3840

LANES = 128
VMEM_LIMIT_BYTES = 56 * 1024 * 1024

LOG2E = 1.4426950408889634
Q_SCALE = HEAD_DIM ** -0.5 * LOG2E
NEG_BIG = -1e30

TOKEN_TILE = 512
NA_ROWS_PER_STEP = 32
GQA_Q_TILE = 512
DIFF_Q_TILE = 512
CHUNKS_PER_ITER = 4
SAFE_EXCESS = 64.0
PROBE_KEYS = 128


def _const_spec(shape):
    nd = len(shape)
    return pl.BlockSpec(shape, lambda *_: (0,) * nd, pipeline_mode=pl.Buffered(1))


def _params(*sem):
    return pltpu.CompilerParams(dimension_semantics=sem, vmem_limit_bytes=VMEM_LIMIT_BYTES)


def _rms(x, eps=RMS_EPS):
    return x * lax.rsqrt(jnp.mean(x * x, axis=-1, keepdims=True) + eps)


def _permute_columns(w):
    d = w.shape[0]
    na = w[:, 0:1536]
    qb = w[:, 1536:2048].reshape(d, 2, 4, 2, 2, 16).transpose(0, 1, 4, 2, 3, 5).reshape(d, 512)
    kb = w[:, 2048:2176].reshape(d, 2, 2, 2, 16).transpose(0, 1, 3, 2, 4)
    kb = jnp.broadcast_to(kb[:, :, :, None], (d, 2, 2, 4, 2, 16)).reshape(d, 512)
    vb = w[:, 2176:2304]
    qc, kc = (w[:, c:c + 512].reshape(d, 2, 4, 2, 32).transpose(0, 1, 3, 2, 4).reshape(d, 512)
              for c in (2304, 2816))
    vc = w[:, 3328:3840]
    return jnp.concatenate([na, qb, kb, vb, qc, kc, vc], axis=1)


_C_QA, _C_KA, _C_VA, _C_QB, _C_KB, _C_VB, _C_QC, _C_KC, _C_VC, _C_END = (
    0, 512, 1024, 1536, 2048, 2560, 2688, 3200, 3712, 4224)


def _gqa_gain_tiles(g):
    idx = np.zeros((2, 128), np.int32)
    for slot in range(4):
        for ab in range(2):
            for i in range(16):
                idx[0, slot * 32 + ab * 16 + i] = ab * 32 + i
                idx[1, slot * 32 + ab * 16 + i] = ab * 32 + 16 + i
    return g.astype(F32)[idx]


def _rope_tables(n):
    t = jnp.arange(n)
    inv32 = 1.0 / (ROPE_THETA ** (jnp.arange(0, 32, 2, dtype=F32) / 32))
    inv64 = 1.0 / (ROPE_THETA ** (jnp.arange(0, 64, 2, dtype=F32) / 64))
    ang_r = (t // GRID_W).astype(F32)[:, None] * inv32[None, :]
    ang_c = (t % GRID_W).astype(F32)[:, None] * inv32[None, :]
    ang_g = jnp.tile(jnp.concatenate([ang_r, ang_c], axis=-1), (1, 4))
    ang_d = jnp.tile(t.astype(F32)[:, None] * inv64[None, :], (1, 4))
    return jnp.cos(ang_g), jnp.sin(ang_g), jnp.cos(ang_d), jnp.sin(ang_d)


def _inproj_kernel(x_ref, g_ref, w_ref, cg_ref, sg_ref, cd_ref, sd_ref, gq_ref, gk_ref,
                   qa_ref, ka_ref, va_ref, qb_ref, kb_ref, vb_ref, qc_ref, kc_ref, vc_ref):
    h = (_rms(x_ref[...]) * g_ref[...]).astype(BF16)

    def proj(c0, c1):
        return jnp.dot(h, w_ref[:, c0:c1], preferred_element_type=F32)

    qa_ref[...] = (proj(_C_QA, _C_KA) * Q_SCALE).astype(BF16)
    ka_ref[...] = proj(_C_KA, _C_VA).astype(BF16)
    va_ref[...] = proj(_C_VA, _C_QB).astype(BF16)
    vt = proj(_C_VB, _C_QC).T
    ones = jnp.ones((GQA_VT_ROWS - HEAD_DIM, vt.shape[1]), F32)
    vb_ref[0] = jnp.concatenate([vt[0:HEAD_DIM], ones, vt[HEAD_DIM:], ones], axis=0).astype(BF16)
    vct = proj(_C_VC, _C_END).T
    ones = jnp.ones((DIFF_VT_ROWS - DIFF_VDIM, vct.shape[1]), F32)
    vc_ref[0] = jnp.concatenate(
        [piece for h in range(DIFF_HEADS) for piece in (vct[h * DIFF_VDIM:(h + 1) * DIFF_VDIM], ones)],
        axis=0).astype(BF16)

    slot = lax.broadcasted_iota(jnp.int32, (1, LANES), 1) // 32

    def rope(x1, x2, c, s):
        return x1 * c - x2 * s, x2 * c + x1 * s

    cg, sg = cg_ref[...], sg_ref[...]

    def gqa(y, gains_ref, out_ref, per_slot, scale):
        for grp in range(2):
            p = y[:, grp * 256:grp * 256 + 128]
            q = y[:, grp * 256 + 128:grp * 256 + 256]
            ssq = p * p + q * q
            if per_slot:
                r = jnp.zeros_like(ssq)
                for sl in range(4):
                    tot = jnp.sum(jnp.where(slot == sl, ssq, 0.0), axis=-1, keepdims=True)
                    r = jnp.where(slot == sl, lax.rsqrt(tot * (1.0 / HEAD_DIM) + RMS_EPS), r)
            else:
                tot = jnp.sum(jnp.where(slot == 0, ssq, 0.0), axis=-1, keepdims=True)
                r = lax.rsqrt(tot * (1.0 / HEAD_DIM) + RMS_EPS)
            o1, o2 = rope(p * r * gains_ref[0:1, :], q * r * gains_ref[1:2, :], cg, sg)
            out_ref[:, grp * 256:grp * 256 + 128] = (o1 * scale).astype(BF16)
            out_ref[:, grp * 256 + 128:grp * 256 + 256] = (o2 * scale).astype(BF16)

    gqa(proj(_C_QB, _C_KB), gq_ref, qb_ref, True, Q_SCALE)
    gqa(proj(_C_KB, _C_VB), gk_ref, kb_ref, False, 1.0)

    cd, sd = cd_ref[...], sd_ref[...]

    def diff(y, out_ref, scale):
        for grp in range(2):
            o1, o2 = rope(y[:, grp * 256:grp * 256 + 128], y[:, grp * 256 + 128:grp * 256 + 256], cd, sd)
            out_ref[:, grp * 256:grp * 256 + 128] = (o1 * scale).astype(BF16)
            out_ref[:, grp * 256 + 128:grp * 256 + 256] = (o2 * scale).astype(BF16)

    diff(proj(_C_QC, _C_KC), qc_ref, Q_SCALE)
    diff(proj(_C_KC, _C_VC), kc_ref, 1.0)


def _inproj(x2d, n_seq, gain, w_perm, tables, gq, gk):
    t = x2d.shape[0]
    tm = TOKEN_TILE
    assert t % tm == 0 and n_seq % tm == 0
    seq_tiles = n_seq // tm
    tok = lambda c: pl.BlockSpec((tm, c), lambda i: (i, 0))
    tab = pl.BlockSpec((tm, LANES), lambda i: (i % seq_tiles, 0))
    widths = (512, 512, 512, 512, 512, -GQA_KV_HEADS * GQA_VT_ROWS, 512, 512,
              -DIFF_HEADS * DIFF_VT_ROWS)
    spec = lambda c: tok(c) if c > 0 else pl.BlockSpec((1, -c, tm), lambda i: (i, 0, 0))
    shape = lambda c: jax.ShapeDtypeStruct((t, c) if c > 0 else (t // tm, -c, tm), BF16)
    return pl.pallas_call(
        _inproj_kernel,
        grid=(t // tm,),
        in_specs=[tok(D_MODEL), _const_spec((1, D_MODEL)), _const_spec((D_MODEL, _C_END)),
                  tab, tab, tab, tab, _const_spec((2, LANES)), _const_spec((2, LANES))],
        out_specs=[spec(c) for c in widths],
        out_shape=[shape(c) for c in widths],
        compiler_params=_params("parallel"),
        name="inproj",
    )(x2d, gain, w_perm, *tables, gq, gk)


def _na_bias_table(rpb):
    qc = np.arange(GRID_W)[:, None]
    kc = np.arange(GRID_W)[None, :]
    w_start = np.clip(qc - NA_KW // 2, 0, GRID_W - NA_KW)
    valid = (kc >= w_start) & (kc < w_start + NA_KW)
    dc = np.clip(kc - qc, -(NA_KW - 1), NA_KW - 1) + NA_KW - 1
    b = rpb.astype(F32)[:, :, dc]
    b = jnp.where(valid[None, None], b * LOG2E, NEG_BIG)
    b = b.reshape(NA_HEADS // 2, 2, NA_BIAS_ROWS, GRID_W, GRID_W)
    b = b.transpose(0, 2, 4, 1, 3)
    return b.reshape(NA_HEADS // 2, NA_BIAS_ROWS * GRID_W, 2 * GRID_W)


def _na_kernel(q_ref, k_ref, v_ref, b_ref, o_ref, *, rows):
    c = pl.program_id(2)
    lo = lax.broadcasted_iota(jnp.int32, (1, LANES), 1) < HEAD_DIM
    win_tokens = NA_MAX_KH * GRID_W
    rows_per_step = q_ref.shape[0] // GRID_W

    def window(i):
        r = c * rows_per_step + i
        r_start = jnp.clip(r - NA_MAX_KH // 2, 0, rows - NA_MAX_KH)
        return r - r_start, pl.multiple_of(r_start * GRID_W, GRID_W)

    def scores(i):
        _, t0 = window(i)
        q = q_ref[i * GRID_W:(i + 1) * GRID_W, :]
        zero = jnp.zeros_like(q)
        qq = jnp.concatenate([jnp.where(lo, q, zero), jnp.where(lo, zero, q)], axis=0)
        return _nt_dot(k_ref[pl.ds(t0, win_tokens), :], qq)

    ones = jnp.ones((win_tokens, LANES), BF16)

    def attend(i, st):
        off, t0 = window(i)
        b0 = pl.multiple_of((NA_MAX_KH - 1 - off) * GRID_W, GRID_W)
        st = st + b_ref[0, pl.ds(b0, win_tokens), :]
        pt = jnp.exp2(st - jnp.max(st, axis=0, keepdims=True)).T.astype(BF16)
        v1 = jnp.concatenate([v_ref[pl.ds(t0, win_tokens), :], ones], axis=1)
        o2 = jnp.dot(pt, v1, preferred_element_type=F32)
        o2 = o2[:, 0:LANES] * (1.0 / o2[:, LANES:])
        o_ref[i * GRID_W:(i + 1) * GRID_W, :] = jnp.where(lo, o2[0:GRID_W], o2[GRID_W:]).astype(BF16)

    _skewed(list(range(rows_per_step)), scores, attend)


def _na_attention(q, k, v, bias, batch, n_seq):
    rows = n_seq // GRID_W
    assert rows >= NA_MAX_KH
    chunk = math.gcd(rows, NA_ROWS_PER_STEP) * GRID_W
    chunks = n_seq // chunk
    qspec = pl.BlockSpec((chunk, LANES), lambda b, hp, c: (b * chunks + c, hp))
    kvspec = pl.BlockSpec((n_seq, LANES), lambda b, hp, c: (b, hp))
    bspec = pl.BlockSpec((1, NA_BIAS_ROWS * GRID_W, 2 * GRID_W), lambda b, hp, c: (hp, 0, 0))
    return pl.pallas_call(
        functools.partial(_na_kernel, rows=rows),
        grid=(batch, NA_HEADS // 2, chunks),
        in_specs=[qspec, kvspec, kvspec, bspec],
        out_specs=qspec,
        out_shape=jax.ShapeDtypeStruct(q.shape, BF16),
        compiler_params=_params("parallel", "parallel", "arbitrary"),
        name="na_attn",
    )(q, k, v, bias)


def _skewed(items, first, second):
    pending = first(items[0])
    for i, item in enumerate(items):
        upcoming = first(items[i + 1]) if i + 1 < len(items) else None
        second(item, pending)
        pending = upcoming


def _nt_dot(a, b):
    return lax.dot_general(a, b, (((1,), (1,)), ((), ())), preferred_element_type=F32)


def _exp2_bf16(x):
    return jnp.exp2(x.astype(BF16))


def _streaming_softmax_matmul(n_chunks, heads, scores, values_t, m_sc, d_sc, acc_sc, write_out):
    acc_sc[...] = jnp.zeros_like(acc_sc)
    d_sc[...] = jnp.full_like(d_sc, -jnp.inf)
    per_iter = math.gcd(n_chunks, CHUNKS_PER_ITER)

    def step(it, carry):
        def first(item):
            c, h = item
            return scores(it * per_iter + c, h)

        def second(item, st):
            c, h = item
            d_sc[h] = jnp.maximum(d_sc[h], jnp.max(st.reshape(-1, 8, st.shape[-1]), axis=0))
            acc_sc[h] += jnp.dot(values_t(it * per_iter + c, h), _exp2_bf16(st - m_sc[h]),
                                 preferred_element_type=F32)

        _skewed([(c, h) for c in range(per_iter) for h in range(heads)], first, second)
        return carry

    lax.fori_loop(0, n_chunks // per_iter, step, 0)
    write_out()

    excess = None
    for h in range(heads):
        e = jnp.max(d_sc[h], axis=0, keepdims=True) - m_sc[h]
        excess = e if excess is None else jnp.maximum(excess, e)

    @pl.when(jnp.max(excess) > SAFE_EXCESS)
    def _():
        acc_sc[...] = jnp.zeros_like(acc_sc)
        m_sc[...] = jnp.full_like(m_sc, -jnp.inf)

        def chunk(c, carry):
            for h in range(heads):
                st = scores(c, h)
                m_prev = m_sc[h]
                m_new = jnp.maximum(m_prev, jnp.max(st, axis=0, keepdims=True))
                acc_sc[h] = (jnp.exp2(m_prev - m_new) * acc_sc[h]
                             + jnp.dot(values_t(c, h), _exp2_bf16(st - m_new),
                                       preferred_element_type=F32))
                m_sc[h] = m_new
            return carry

        lax.fori_loop(0, n_chunks, chunk, 0)
        write_out()


def _group_cols(h):
    return slice((h // 4) * 2 * LANES, (h // 4 + 1) * 2 * LANES)


def _prepare_queries(q_ref, k_ref, qm_sc, m_sc):
    qt = q_ref[...].astype(F32).T
    slot = (lax.broadcasted_iota(jnp.int32, (2 * LANES, 1), 0) % LANES) // 32
    for h in range(qm_sc.shape[0]):
        qg = qt[_group_cols(h)]
        qm_sc[h] = jnp.where(slot == h % 4, qg, 0.0).astype(BF16)
        probe = jnp.dot(k_ref[0:PROBE_KEYS, _group_cols(h)], qm_sc[h], preferred_element_type=F32)
        m_sc[h] = jnp.max(probe, axis=0, keepdims=True)


def _chunk_scores(k_ref, qm_sc, c, h):
    t0 = pl.multiple_of(c * TOKEN_TILE, TOKEN_TILE)
    return jnp.dot(k_ref[pl.ds(t0, TOKEN_TILE), _group_cols(h)], qm_sc[h],
                   preferred_element_type=F32)


def _gqa_kernel(q_ref, k_ref, vt_ref, o_ref, qm_sc, m_sc, d_sc, acc_sc):
    _prepare_queries(q_ref, k_ref, qm_sc, m_sc)
    scores = functools.partial(_chunk_scores, k_ref, qm_sc)

    def values_t(c, h):
        return vt_ref[c, (h // 4) * GQA_VT_ROWS:(h // 4 + 1) * GQA_VT_ROWS, :]

    def write_out():
        for hp in range(GQA_Q_HEADS // 2):
            halves = []
            for h in (2 * hp, 2 * hp + 1):
                acc = acc_sc[h]
                halves.append(acc[0:HEAD_DIM] * (1.0 / acc[HEAD_DIM:HEAD_DIM + 1]))
            o_ref[:, hp * LANES:(hp + 1) * LANES] = jnp.concatenate(halves, axis=0).T.astype(BF16)

    _streaming_softmax_matmul(vt_ref.shape[0], GQA_Q_HEADS, scores, values_t, m_sc, d_sc, acc_sc,
                              write_out)


def _gqa_attention(q, k, vt, batch, n_seq):
    tq = GQA_Q_TILE
    assert n_seq % tq == 0 and n_seq % TOKEN_TILE == 0
    nq, n_chunks = n_seq // tq, n_seq // TOKEN_TILE
    return pl.pallas_call(
        _gqa_kernel,
        grid=(batch, nq),
        in_specs=[pl.BlockSpec((tq, BRANCH_WIDTH), lambda b, i: (b * nq + i, 0)),
                  pl.BlockSpec((n_seq, BRANCH_WIDTH), lambda b, i: (b, 0)),
                  pl.BlockSpec((n_chunks, GQA_KV_HEADS * GQA_VT_ROWS, TOKEN_TILE),
                               lambda b, i: (b, 0, 0))],
        out_specs=pl.BlockSpec((tq, BRANCH_WIDTH), lambda b, i: (b * nq + i, 0)),
        out_shape=jax.ShapeDtypeStruct((batch * n_seq, BRANCH_WIDTH), BF16),
        scratch_shapes=[pltpu.VMEM((GQA_Q_HEADS, 2 * LANES, tq), BF16),
                        pltpu.VMEM((GQA_Q_HEADS, 1, tq), F32),
                        pltpu.VMEM((GQA_Q_HEADS, 8, tq), F32),
                        pltpu.VMEM((GQA_Q_HEADS, GQA_VT_ROWS, tq), F32)],
        compiler_params=_params("parallel", "arbitrary"),
        name="gqa_attn",
    )(q, k, vt)


def _diff_kernel(q_ref, k_ref, vt_ref, lam_ref, g_ref, o_ref, qm_sc, m_sc, d_sc, acc_sc, *, lam_init):
    subheads = 2 * DIFF_HEADS
    _prepare_queries(q_ref, k_ref, qm_sc, m_sc)
    scores = functools.partial(_chunk_scores, k_ref, qm_sc)

    def values_t(c, j):
        return vt_ref[c, (j // 2) * DIFF_VT_ROWS:(j // 2 + 1) * DIFF_VT_ROWS, :]

    lp = lam_ref[...]
    lam = (jnp.exp(jnp.sum(lp[0:1] * lp[1:2], axis=-1, keepdims=True))
           - jnp.exp(jnp.sum(lp[2:3] * lp[3:4], axis=-1, keepdims=True)) + lam_init)

    def write_out():
        for h in range(DIFF_HEADS):
            a1, a2 = acc_sc[2 * h], acc_sc[2 * h + 1]
            ot = (a1[0:DIFF_VDIM] * (1.0 / a1[DIFF_VDIM:DIFF_VDIM + 1])
                  - a2[0:DIFF_VDIM] * (lam / a2[DIFF_VDIM:DIFF_VDIM + 1]))
            o = _rms(ot.T) * g_ref[...] * (1.0 - lam_init)
            o_ref[:, h * DIFF_VDIM:(h + 1) * DIFF_VDIM] = o.astype(BF16)

    _streaming_softmax_matmul(vt_ref.shape[0], subheads, scores, values_t, m_sc, d_sc, acc_sc,
                              write_out)


def _diff_attention(q, k, vt, lam_params, subln_g, lam_init, batch, n_seq):
    tq = DIFF_Q_TILE
    assert n_seq % tq == 0 and n_seq % TOKEN_TILE == 0
    nq, n_chunks = n_seq // tq, n_seq // TOKEN_TILE
    subheads = 2 * DIFF_HEADS
    return pl.pallas_call(
        functools.partial(_diff_kernel, lam_init=lam_init),
        grid=(batch, nq),
        in_specs=[pl.BlockSpec((tq, BRANCH_WIDTH), lambda b, i: (b * nq + i, 0)),
                  pl.BlockSpec((n_seq, BRANCH_WIDTH), lambda b, i: (b, 0)),
                  pl.BlockSpec((n_chunks, DIFF_HEADS * DIFF_VT_ROWS, TOKEN_TILE),
                               lambda b, i: (b, 0, 0)),
                  _const_spec((4, HEAD_DIM)), _const_spec((1, DIFF_VDIM))],
        out_specs=pl.BlockSpec((tq, BRANCH_WIDTH), lambda b, i: (b * nq + i, 0)),
        out_shape=jax.ShapeDtypeStruct((batch * n_seq, BRANCH_WIDTH), BF16),
        scratch_shapes=[pltpu.VMEM((subheads, 2 * LANES, tq), BF16),
                        pltpu.VMEM((subheads, 1, tq), F32),
                        pltpu.VMEM((subheads, 8, tq), F32),
                        pltpu.VMEM((subheads, DIFF_VT_ROWS, tq), F32)],
        compiler_params=_params("parallel", "arbitrary"),
        name="diff_attn",
    )(q, k, vt, lam_params, subln_g)


def _merge_kernel(x_ref, g_ref, oa_ref, ob_ref, oc_ref, wg_ref, wb_ref, wo_ref, y_ref):
    x = x_ref[...]
    h = (_rms(x) * g_ref[...]).astype(BF16)
    merged = None
    for i, o_ref in enumerate((oa_ref, ob_ref, oc_ref)):
        gate = jax.nn.sigmoid(jnp.dot(h, wg_ref[:, i * D_MODEL:(i + 1) * D_MODEL],
                                      preferred_element_type=F32))
        term = gate * jnp.dot(o_ref[...], wb_ref[i], preferred_element_type=F32)
        merged = term if merged is None else merged + term
    y_ref[...] = x + jnp.dot(merged.astype(BF16), wo_ref[...], preferred_element_type=F32)


def _merge(x2d, gain, oa, ob, oc, w_gate, w_branch, w_out):
    t = x2d.shape[0]
    tm = TOKEN_TILE
    tok = lambda c: pl.BlockSpec((tm, c), lambda i: (i, 0))
    return pl.pallas_call(
        _merge_kernel,
        grid=(t // tm,),
        in_specs=[tok(D_MODEL), _const_spec((1, D_MODEL)), tok(BRANCH_WIDTH), tok(BRANCH_WIDTH),
                  tok(BRANCH_WIDTH), _const_spec((D_MODEL, 3 * D_MODEL)),
                  _const_spec((3, BRANCH_WIDTH, D_MODEL)), _const_spec((D_MODEL, D_MODEL))],
        out_specs=tok(D_MODEL),
        out_shape=jax.ShapeDtypeStruct((t, D_MODEL), F32),
        compiler_params=_params("parallel"),
        name="merge",
    )(x2d, gain, oa, ob, oc, w_gate, w_branch, w_out)


FFN_CHUNKS = 2


def _ffn_kernel(x_ref, g_ref, wg_ref, wu_ref, wd_ref, gf_ref, y_ref, *, final_norm):
    x = x_ref[...]
    h = (_rms(x) * g_ref[...]).astype(BF16)
    width = D_FF // FFN_CHUNKS
    y = x
    for c in range(FFN_CHUNKS):
        cs = slice(c * width, (c + 1) * width)
        a = jnp.dot(h, wg_ref[:, cs], preferred_element_type=F32)
        b = jnp.dot(h, wu_ref[:, cs], preferred_element_type=F32)
        act = (a * jax.nn.sigmoid(a) * b).astype(BF16)
        y = y + jnp.dot(act, wd_ref[cs, :], preferred_element_type=F32)
    if final_norm:
        y = _rms(y) * gf_ref[...]
    y_ref[...] = y


def _ffn(x2d, gain, w_gate, w_up, w_down, gain_final, final_norm):
    t = x2d.shape[0]
    tm = TOKEN_TILE
    tok = pl.BlockSpec((tm, D_MODEL), lambda i: (i, 0))
    return pl.pallas_call(
        functools.partial(_ffn_kernel, final_norm=final_norm),
        grid=(t // tm,),
        in_specs=[tok, _const_spec((1, D_MODEL)), _const_spec((D_MODEL, D_FF)),
                  _const_spec((D_MODEL, D_FF)), _const_spec((D_FF, D_MODEL)),
                  _const_spec((1, D_MODEL))],
        out_specs=tok,
        out_shape=jax.ShapeDtypeStruct((t, D_MODEL), F32),
        compiler_params=_params("parallel"),
        name="ffn",
    )(x2d, gain, w_gate, w_up, w_down, gain_final)


def _prepare_layer(l, norm_mix, w_in, na_rpb, qk_norm, diff_lambda, diff_subln, w_branch, w_out,
                   norm_ffn, w_ffn_gate, w_ffn_up, w_ffn_down):
    row = lambda g: g.astype(F32).reshape(1, -1)
    return dict(
        lam_init=0.8 - 0.6 * math.exp(-0.3 * l),
        norm_mix=row(norm_mix[l]),
        w_qkv=_permute_columns(w_in[l][:, :QKV_COLS].astype(BF16)),
        w_gate=w_in[l][:, QKV_COLS:].astype(BF16),
        na_bias=_na_bias_table(na_rpb[l]),
        gq=_gqa_gain_tiles(qk_norm[l, 0]),
        gk=_gqa_gain_tiles(qk_norm[l, 1]),
        diff_lambda=diff_lambda[l].astype(F32),
        diff_subln=row(diff_subln[l]),
        w_branch=w_branch[l].astype(BF16),
        w_out=w_out[l].astype(BF16),
        norm_ffn=row(norm_ffn[l]),
        w_ffn_gate=w_ffn_gate[l].astype(BF16),
        w_ffn_up=w_ffn_up[l].astype(BF16),
        w_ffn_down=w_ffn_down[l].astype(BF16),
    )


def _trunk(x, layers, norm_final):
    batch, n_seq, _ = x.shape
    x2d = x.reshape(batch * n_seq, D_MODEL)
    tables = _rope_tables(n_seq)
    gf = norm_final.astype(F32).reshape(1, -1)
    for l, p in enumerate(layers):
        qa, ka, va, qb, kb, vb, qc, kc, vc = _inproj(x2d, n_seq, p["norm_mix"], p["w_qkv"], tables,
                                                     p["gq"], p["gk"])
        o_a = _na_attention(qa, ka, va, p["na_bias"], batch, n_seq)
        o_b = _gqa_attention(qb, kb, vb, batch, n_seq)
        o_c = _diff_attention(qc, kc, vc, p["diff_lambda"], p["diff_subln"], p["lam_init"], batch, n_seq)
        x2d = _merge(x2d, p["norm_mix"], o_a, o_b, o_c, p["w_gate"], p["w_branch"], p["w_out"])
        x2d = _ffn(x2d, p["norm_ffn"], p["w_ffn_gate"], p["w_ffn_up"], p["w_ffn_down"], gf,
                   final_norm=(l == len(layers) - 1))
    return x2d.reshape(batch, n_seq, D_MODEL)


def kernel(x_prompt, x_sample, norm_mix, w_in, na_rpb, qk_norm, diff_lambda, diff_subln, w_branch,
           w_out, norm_ffn, w_ffn_gate, w_ffn_up, w_ffn_down, norm_final):
    depth = w_in.shape[0]
    layers = [_prepare_layer(l, norm_mix, w_in, na_rpb, qk_norm, diff_lambda, diff_subln, w_branch,
                             w_out, norm_ffn, w_ffn_gate, w_ffn_up, w_ffn_down) for l in range(depth)]
    return (_trunk(x_prompt, layers, norm_final), _trunk(x_sample, layers, norm_final))
```

```python
import functools
import math

import jax
import jax.numpy as jnp
import numpy as np
from jax import lax
from jax.experimental import pallas as pl
from jax.experimental.pallas import tpu as pltpu

F32 = jnp.float32
BF16 = jnp.bfloat16

D_MODEL = 1024
GRID_W = 64
HEAD_DIM = 64
NA_HEADS = 8
NA_MAX_KH = 8
NA_KW = 16
NA_BIAS_ROWS = 2 * NA_MAX_KH - 1
GQA_Q_HEADS = 8
GQA_KV_HEADS = 2
DIFF_HEADS = 4
DIFF_VDIM = 128
BRANCH_WIDTH = 512
ROPE_THETA = 10000.0
RMS_EPS = 1e-6
D_FF = 2816
BF16_SUBLANES = 16
GQA_VT_ROWS = HEAD_DIM + BF16_SUBLANES
DIFF_VT_ROWS = DIFF_VDIM + BF16_SUBLANES
QKV_COLS = 3840

LANES = 128
VMEM_LIMIT_BYTES = 56 * 1024 * 1024

LOG2E = 1.4426950408889634
Q_SCALE = HEAD_DIM ** -0.5 * LOG2E
NEG_BIG = -1e30

TOKEN_TILE = 512
NA_ROWS_PER_STEP = 32
GQA_Q_TILE = 512
DIFF_Q_TILE = 512
CHUNKS_PER_ITER = 4
SAFE_EXCESS = 64.0
PROBE_KEYS = 128


def _const_spec(shape):
    nd = len(shape)
    return pl.BlockSpec(shape, lambda *_: (0,) * nd, pipeline_mode=pl.Buffered(1))


def _params(*sem):
    return pltpu.CompilerParams(dimension_semantics=sem, vmem_limit_bytes=VMEM_LIMIT_BYTES)


def _rms(x, eps=RMS_EPS):
    return x * lax.rsqrt(jnp.mean(x * x, axis=-1, keepdims=True) + eps)


def _permute_columns(w):
    d = w.shape[0]
    na = w[:, 0:1536]
    qb = w[:, 1536:2048].reshape(d, 2, 4, 2, 2, 16).transpose(0, 2, 4, 1, 3, 5).reshape(d, 512)
    kb = w[:, 2048:2176].reshape(d, 2, 2, 2, 16).transpose(0, 3, 1, 2, 4).reshape(d, 128)
    vb = w[:, 2176:2304]
    qc, kc = (w[:, c:c + 512].reshape(d, 4, 2, 2, 32).transpose(0, 1, 3, 2, 4).reshape(d, 512)
              for c in (2304, 2816))
    vc = w[:, 3328:3840]
    return jnp.concatenate([na, qb, kb, vb, qc, kc, vc], axis=1)


_C_QA, _C_KA, _C_VA, _C_QB, _C_KB, _C_VB, _C_QC, _C_KC, _C_VC, _C_END = (
    0, 512, 1024, 1536, 2048, 2176, 2304, 2816, 3328, 3840)


def _gqa_gain_tile(g):
    idx = np.zeros((1, LANES), np.int32)
    for x12 in range(2):
        for e in range(2):
            for ab in range(2):
                for i in range(16):
                    idx[0, x12 * 64 + e * 32 + ab * 16 + i] = ab * 32 + x12 * 16 + i
    return g.astype(F32)[idx]


def _rope_tables(n):
    t = jnp.arange(n)
    inv32 = 1.0 / (ROPE_THETA ** (jnp.arange(0, 32, 2, dtype=F32) / 32))
    inv64 = 1.0 / (ROPE_THETA ** (jnp.arange(0, 64, 2, dtype=F32) / 64))
    ang_r = (t // GRID_W).astype(F32)[:, None] * inv32[None, :]
    ang_c = (t % GRID_W).astype(F32)[:, None] * inv32[None, :]
    ang_g = jnp.tile(jnp.concatenate([ang_r, ang_c], axis=-1), (1, 4))
    ang_d = jnp.tile(t.astype(F32)[:, None] * inv64[None, :], (1, 4))
    sign = jnp.where(jnp.arange(LANES) < LANES // 2, -1.0, 1.0).astype(F32)[None, :]
    return jnp.cos(ang_g), jnp.sin(ang_g) * sign, jnp.cos(ang_d), jnp.sin(ang_d) * sign


def _inproj_kernel(x_ref, g_ref, w_ref, cg_ref, sg_ref, cd_ref, sd_ref, gq_ref, gk_ref,
                   qa_ref, ka_ref, va_ref, qb_ref, kb_ref, vb_ref, qc_ref, kc_ref, vc_ref):
    h = (_rms(x_ref[...]) * g_ref[...]).astype(BF16)

    def proj(c0, c1):
        return jnp.dot(h, w_ref[:, c0:c1], preferred_element_type=F32)

    pair = (lax.broadcasted_iota(jnp.int32, (1, LANES), 1) % (LANES // 2)) // 32

    def rope(x, c, s):
        return x * c + pltpu.roll(x, LANES // 2, 1) * s

    def tiles(y):
        return [y[:, c:c + LANES] for c in range(0, y.shape[1], LANES)]

    cg, sg = cg_ref[...], sg_ref[...]

    def gqa(y, gains_ref, out_ref, scale):
        for t, x in enumerate(tiles(y)):
            ssq = x * x
            r = jnp.zeros_like(x)
            for e in range(2):
                tot = jnp.sum(jnp.where(pair == e, ssq, 0.0), axis=-1, keepdims=True)
                r = jnp.where(pair == e, lax.rsqrt(tot * (1.0 / HEAD_DIM) + RMS_EPS), r)
            out = rope(x * r * gains_ref[...], cg, sg) * scale
            out_ref[:, t * LANES:(t + 1) * LANES] = out.astype(BF16)

    gqa(proj(_C_QB, _C_KB), gq_ref, qb_ref, Q_SCALE)
    gqa(proj(_C_KB, _C_VB), gk_ref, kb_ref, 1.0)

    cd, sd = cd_ref[...], sd_ref[...]

    def diff(y, out_ref, scale):
        for t, x in enumerate(tiles(y)):
            out_ref[:, t * LANES:(t + 1) * LANES] = (rope(x, cd, sd) * scale).astype(BF16)

    diff(proj(_C_QC, _C_KC), qc_ref, Q_SCALE)
    diff(proj(_C_KC, _C_VC), kc_ref, 1.0)

    vt = proj(_C_VB, _C_QC).T
    ones = jnp.ones((GQA_VT_ROWS - HEAD_DIM, vt.shape[1]), F32)
    vb_ref[0] = jnp.concatenate([vt[0:HEAD_DIM], ones, vt[HEAD_DIM:], ones], axis=0).astype(BF16)
    vct = proj(_C_VC, _C_END).T
    ones = jnp.ones((DIFF_VT_ROWS - DIFF_VDIM, vct.shape[1]), F32)
    vc_ref[0] = jnp.concatenate(
        [piece for h in range(DIFF_HEADS) for piece in (vct[h * DIFF_VDIM:(h + 1) * DIFF_VDIM], ones)],
        axis=0).astype(BF16)

    qa_ref[...] = (proj(_C_QA, _C_KA) * Q_SCALE).astype(BF16)
    ka_ref[...] = proj(_C_KA, _C_VA).astype(BF16)
    va_ref[...] = proj(_C_VA, _C_QB).astype(BF16)


def _inproj(x2d, n_seq, gain, w_perm, tables, gq, gk):
    t = x2d.shape[0]
    tm = TOKEN_TILE
    assert t % tm == 0 and n_seq % tm == 0
    seq_tiles = n_seq // tm
    tok = lambda c: pl.BlockSpec((tm, c), lambda i: (i, 0))
    tab = pl.BlockSpec((tm, LANES), lambda i: (i % seq_tiles, 0))
    widths = (512, 512, 512, 512, LANES, -GQA_KV_HEADS * GQA_VT_ROWS, 512, 512,
              -DIFF_HEADS * DIFF_VT_ROWS)
    spec = lambda c: tok(c) if c > 0 else pl.BlockSpec((1, -c, tm), lambda i: (i, 0, 0))
    shape = lambda c: jax.ShapeDtypeStruct((t, c) if c > 0 else (t // tm, -c, tm), BF16)
    return pl.pallas_call(
        _inproj_kernel,
        grid=(t // tm,),
        in_specs=[tok(D_MODEL), _const_spec((1, D_MODEL)), _const_spec((D_MODEL, _C_END)),
                  tab, tab, tab, tab, _const_spec((1, LANES)), _const_spec((1, LANES))],
        out_specs=[spec(c) for c in widths],
        out_shape=[shape(c) for c in widths],
        compiler_params=_params("parallel"),
        name="inproj",
    )(x2d, gain, w_perm, *tables, gq, gk)


def _na_bias_table(rpb):
    qc = np.arange(GRID_W)[:, None]
    kc = np.arange(GRID_W)[None, :]
    w_start = np.clip(qc - NA_KW // 2, 0, GRID_W - NA_KW)
    valid = (kc >= w_start) & (kc < w_start + NA_KW)
    dc = np.clip(kc - qc, -(NA_KW - 1), NA_KW - 1) + NA_KW - 1
    b = rpb.astype(F32)[:, :, dc]
    b = jnp.where(valid[None, None], b * LOG2E, NEG_BIG)
    b = b.reshape(NA_HEADS // 2, 2, NA_BIAS_ROWS, GRID_W, GRID_W)
    b = b.transpose(0, 2, 4, 1, 3)
    return b.reshape(NA_HEADS // 2, NA_BIAS_ROWS * GRID_W, 2 * GRID_W)


def _na_kernel(q_ref, k_ref, v_ref, b_ref, o_ref, *, rows):
    c = pl.program_id(2)
    lo = lax.broadcasted_iota(jnp.int32, (1, LANES), 1) < HEAD_DIM
    win_tokens = NA_MAX_KH * GRID_W
    rows_per_step = q_ref.shape[0] // GRID_W

    def window(i):
        r = c * rows_per_step + i
        r_start = jnp.clip(r - NA_MAX_KH // 2, 0, rows - NA_MAX_KH)
        return r - r_start, pl.multiple_of(r_start * GRID_W, GRID_W)

    def scores(i):
        _, t0 = window(i)
        q = q_ref[i * GRID_W:(i + 1) * GRID_W, :]
        zero = jnp.zeros_like(q)
        qq = jnp.concatenate([jnp.where(lo, q, zero), jnp.where(lo, zero, q)], axis=0)
        return _nt_dot(k_ref[pl.ds(t0, win_tokens), :], qq)

    ones = jnp.ones((win_tokens, LANES), BF16)

    def attend(i, st):
        off, t0 = window(i)
        b0 = pl.multiple_of((NA_MAX_KH - 1 - off) * GRID_W, GRID_W)
        st = st + b_ref[0, pl.ds(b0, win_tokens), :]
        pt = jnp.exp2(st - jnp.max(st, axis=0, keepdims=True)).T.astype(BF16)
        v1 = jnp.concatenate([v_ref[pl.ds(t0, win_tokens), :], ones], axis=1)
        o2 = jnp.dot(pt, v1, preferred_element_type=F32)
        o2 = o2[:, 0:LANES] * (1.0 / o2[:, LANES:])
        o_ref[i * GRID_W:(i + 1) * GRID_W, :] = jnp.where(lo, o2[0:GRID_W], o2[GRID_W:]).astype(BF16)

    _skewed(list(range(rows_per_step)), scores, attend)


def _na_attention(q, k, v, bias, batch, n_seq):
    rows = n_seq // GRID_W
    assert rows >= NA_MAX_KH
    chunk = math.gcd(rows, NA_ROWS_PER_STEP) * GRID_W
    chunks = n_seq // chunk
    qspec = pl.BlockSpec((chunk, LANES), lambda b, hp, c: (b * chunks + c, hp))
    kvspec = pl.BlockSpec((n_seq, LANES), lambda b, hp, c: (b, hp))
    bspec = pl.BlockSpec((1, NA_BIAS_ROWS * GRID_W, 2 * GRID_W), lambda b, hp, c: (hp, 0, 0))
    return pl.pallas_call(
        functools.partial(_na_kernel, rows=rows),
        grid=(batch, NA_HEADS // 2, chunks),
        in_specs=[qspec, kvspec, kvspec, bspec],
        out_specs=qspec,
        out_shape=jax.ShapeDtypeStruct(q.shape, BF16),
        compiler_params=_params("parallel", "parallel", "arbitrary"),
        name="na_attn",
    )(q, k, v, bias)


def _skewed(items, first, second):
    pending = first(items[0])
    for i, item in enumerate(items):
        upcoming = first(items[i + 1]) if i + 1 < len(items) else None
        second(item, pending)
        pending = upcoming


def _nt_dot(a, b):
    return lax.dot_general(a, b, (((1,), (1,)), ((), ())), preferred_element_type=F32)


def _exp2_bf16(x):
    return jnp.exp2(x.astype(BF16))


def _streaming_softmax_matmul(n_chunks, heads, scores, values_t, m_sc, d_sc, acc_sc, write_out):
    acc_sc[...] = jnp.zeros_like(acc_sc)
    d_sc[...] = jnp.full_like(d_sc, -jnp.inf)
    per_iter = math.gcd(n_chunks, CHUNKS_PER_ITER)

    def step(it, carry):
        def first(item):
            c, h = item
            return scores(it * per_iter + c, h)

        def second(item, st):
            c, h = item
            d_sc[h] = jnp.maximum(d_sc[h], jnp.max(st.reshape(-1, 8, st.shape[-1]), axis=0))
            acc_sc[h] += jnp.dot(values_t(it * per_iter + c, h), _exp2_bf16(st - m_sc[h]),
                                 preferred_element_type=F32)

        _skewed([(c, h) for c in range(per_iter) for h in range(heads)], first, second)
        return carry

    lax.fori_loop(0, n_chunks // per_iter, step, 0)
    write_out()

    excess = None
    for h in range(heads):
        e = jnp.max(d_sc[h], axis=0, keepdims=True) - m_sc[h]
        excess = e if excess is None else jnp.maximum(excess, e)

    @pl.when(jnp.max(excess) > SAFE_EXCESS)
    def _():
        acc_sc[...] = jnp.zeros_like(acc_sc)
        m_sc[...] = jnp.full_like(m_sc, -jnp.inf)

        def chunk(c, carry):
            for h in range(heads):
                st = scores(c, h)
                m_prev = m_sc[h]
                m_new = jnp.maximum(m_prev, jnp.max(st, axis=0, keepdims=True))
                acc_sc[h] = (jnp.exp2(m_prev - m_new) * acc_sc[h]
                             + jnp.dot(values_t(c, h), _exp2_bf16(st - m_new),
                                       preferred_element_type=F32))
                m_sc[h] = m_new
            return carry

        lax.fori_loop(0, n_chunks, chunk, 0)
        write_out()


def _tile_cols(t):
    return slice(t * LANES, (t + 1) * LANES)


def _prepare_queries(q_ref, k_ref, qm_sc, m_sc, layout):
    qt = q_ref[...].astype(F32).T
    pair = (lax.broadcasted_iota(jnp.int32, (LANES, 1), 0) % (LANES // 2)) // 32
    for h in range(qm_sc.shape[0]):
        q_tile, e, k_tile = layout(h)
        qm_sc[h] = jnp.where(pair == e, qt[_tile_cols(q_tile)], 0.0).astype(BF16)
        probe = jnp.dot(k_ref[0:PROBE_KEYS, _tile_cols(k_tile)], qm_sc[h], preferred_element_type=F32)
        m_sc[h] = jnp.max(probe, axis=0, keepdims=True)


def _chunk_scores(k_ref, qm_sc, layout, c, h):
    t0 = pl.multiple_of(c * TOKEN_TILE, TOKEN_TILE)
    return jnp.dot(k_ref[pl.ds(t0, TOKEN_TILE), _tile_cols(layout(h)[2])], qm_sc[h],
                   preferred_element_type=F32)


def _gqa_layout(h):
    return h % 4, h // 4, 0


def _diff_layout(j):
    return j // 2, j % 2, j // 2


def _gqa_kernel(q_ref, k_ref, vt_ref, o_ref, qm_sc, m_sc, d_sc, acc_sc):
    _prepare_queries(q_ref, k_ref, qm_sc, m_sc, _gqa_layout)
    scores = functools.partial(_chunk_scores, k_ref, qm_sc, _gqa_layout)

    def values_t(c, h):
        return vt_ref[c, (h // 4) * GQA_VT_ROWS:(h // 4 + 1) * GQA_VT_ROWS, :]

    def write_out():
        for hp in range(GQA_Q_HEADS // 2):
            halves = []
            for h in (2 * hp, 2 * hp + 1):
                acc = acc_sc[h]
                halves.append(acc[0:HEAD_DIM] * (1.0 / acc[HEAD_DIM:HEAD_DIM + 1]))
            o_ref[:, hp * LANES:(hp + 1) * LANES] = jnp.concatenate(halves, axis=0).T.astype(BF16)

    _streaming_softmax_matmul(vt_ref.shape[0], GQA_Q_HEADS, scores, values_t, m_sc, d_sc, acc_sc,
                              write_out)


def _gqa_attention(q, k, vt, batch, n_seq):
    tq = GQA_Q_TILE
    assert n_seq % tq == 0 and n_seq % TOKEN_TILE == 0
    nq, n_chunks = n_seq // tq, n_seq // TOKEN_TILE
    return pl.pallas_call(
        _gqa_kernel,
        grid=(batch, nq),
        in_specs=[pl.BlockSpec((tq, BRANCH_WIDTH), lambda b, i: (b * nq + i, 0)),
                  pl.BlockSpec((n_seq, LANES), lambda b, i: (b, 0)),
                  pl.BlockSpec((n_chunks, GQA_KV_HEADS * GQA_VT_ROWS, TOKEN_TILE),
                               lambda b, i: (b, 0, 0))],
        out_specs=pl.BlockSpec((tq, BRANCH_WIDTH), lambda b, i: (b * nq + i, 0)),
        out_shape=jax.ShapeDtypeStruct((batch * n_seq, BRANCH_WIDTH), BF16),
        scratch_shapes=[pltpu.VMEM((GQA_Q_HEADS, LANES, tq), BF16),
                        pltpu.VMEM((GQA_Q_HEADS, 1, tq), F32),
                        pltpu.VMEM((GQA_Q_HEADS, 8, tq), F32),
                        pltpu.VMEM((GQA_Q_HEADS, GQA_VT_ROWS, tq), F32)],
        compiler_params=_params("parallel", "arbitrary"),
        name="gqa_attn",
    )(q, k, vt)


def _diff_kernel(q_ref, k_ref, vt_ref, lam_ref, g_ref, o_ref, qm_sc, m_sc, d_sc, acc_sc, *, lam_init):
    subheads = 2 * DIFF_HEADS
    _prepare_queries(q_ref, k_ref, qm_sc, m_sc, _diff_layout)
    scores = functools.partial(_chunk_scores, k_ref, qm_sc, _diff_layout)

    def values_t(c, j):
        return vt_ref[c, (j // 2) * DIFF_VT_ROWS:(j // 2 + 1) * DIFF_VT_ROWS, :]

    lp = lam_ref[...]
    lam = (jnp.exp(jnp.sum(lp[0:1] * lp[1:2], axis=-1, keepdims=True))
           - jnp.exp(jnp.sum(lp[2:3] * lp[3:4], axis=-1, keepdims=True)) + lam_init)

    def write_out():
        for h in range(DIFF_HEADS):
            a1, a2 = acc_sc[2 * h], acc_sc[2 * h + 1]
            ot = (a1[0:DIFF_VDIM] * (1.0 / a1[DIFF_VDIM:DIFF_VDIM + 1])
                  - a2[0:DIFF_VDIM] * (lam / a2[DIFF_VDIM:DIFF_VDIM + 1]))
            o = _rms(ot.T) * g_ref[...] * (1.0 - lam_init)
            o_ref[:, h * DIFF_VDIM:(h + 1) * DIFF_VDIM] = o.astype(BF16)

    _streaming_softmax_matmul(vt_ref.shape[0], subheads, scores, values_t, m_sc, d_sc, acc_sc,
                              write_out)


def _diff_attention(q, k, vt, lam_params, subln_g, lam_init, batch, n_seq):
    tq = DIFF_Q_TILE
    assert n_seq % tq == 0 and n_seq % TOKEN_TILE == 0
    nq, n_chunks = n_seq // tq, n_seq // TOKEN_TILE
    subheads = 2 * DIFF_HEADS
    return pl.pallas_call(
        functools.partial(_diff_kernel, lam_init=lam_init),
        grid=(batch, nq),
        in_specs=[pl.BlockSpec((tq, BRANCH_WIDTH), lambda b, i: (b * nq + i, 0)),
                  pl.BlockSpec((n_seq, BRANCH_WIDTH), lambda b, i: (b, 0)),
                  pl.BlockSpec((n_chunks, DIFF_HEADS * DIFF_VT_ROWS, TOKEN_TILE),
                               lambda b, i: (b, 0, 0)),
                  _const_spec((4, HEAD_DIM)), _const_spec((1, DIFF_VDIM))],
        out_specs=pl.BlockSpec((tq, BRANCH_WIDTH), lambda b, i: (b * nq + i, 0)),
        out_shape=jax.ShapeDtypeStruct((batch * n_seq, BRANCH_WIDTH), BF16),
        scratch_shapes=[pltpu.VMEM((subheads, LANES, tq), BF16),
                        pltpu.VMEM((subheads, 1, tq), F32),
                        pltpu.VMEM((subheads, 8, tq), F32),
                        pltpu.VMEM((subheads, DIFF_VT_ROWS, tq), F32)],
        compiler_params=_params("parallel", "arbitrary"),
        name="diff_attn",
    )(q, k, vt, lam_params, subln_g)


def _merge_kernel(x_ref, g_ref, oa_ref, ob_ref, oc_ref, wg_ref, wb_ref, wo_ref, y_ref):
    x = x_ref[...]
    h = (_rms(x) * g_ref[...]).astype(BF16)
    merged = None
    for i, o_ref in enumerate((oa_ref, ob_ref, oc_ref)):
        gate = jax.nn.sigmoid(jnp.dot(h, wg_ref[:, i * D_MODEL:(i + 1) * D_MODEL],
                                      preferred_element_type=F32))
        term = gate * jnp.dot(o_ref[...], wb_ref[i], preferred_element_type=F32)
        merged = term if merged is None else merged + term
    y_ref[...] = x + jnp.dot(merged.astype(BF16), wo_ref[...], preferred_element_type=F32)


def _merge(x2d, gain, oa, ob, oc, w_gate, w_branch, w_out):
    t = x2d.shape[0]
    tm = TOKEN_TILE
    tok = lambda c: pl.BlockSpec((tm, c), lambda i: (i, 0))
    return pl.pallas_call(
        _merge_kernel,
        grid=(t // tm,),
        in_specs=[tok(D_MODEL), _const_spec((1, D_MODEL)), tok(BRANCH_WIDTH), tok(BRANCH_WIDTH),
                  tok(BRANCH_WIDTH), _const_spec((D_MODEL, 3 * D_MODEL)),
                  _const_spec((3, BRANCH_WIDTH, D_MODEL)), _const_spec((D_MODEL, D_MODEL))],
        out_specs=tok(D_MODEL),
        out_shape=jax.ShapeDtypeStruct((t, D_MODEL), F32),
        compiler_params=_params("parallel"),
        name="merge",
    )(x2d, gain, oa, ob, oc, w_gate, w_branch, w_out)


FFN_CHUNKS = 2


def _ffn_kernel(x_ref, g_ref, wg_ref, wu_ref, wd_ref, gf_ref, y_ref, *, final_norm):
    x = x_ref[...]
    h = (_rms(x) * g_ref[...]).astype(BF16)
    width = D_FF // FFN_CHUNKS
    y = x
    for c in range(FFN_CHUNKS):
        cs = slice(c * width, (c + 1) * width)
        a = jnp.dot(h, wg_ref[:, cs], preferred_element_type=F32)
        b = jnp.dot(h, wu_ref[:, cs], preferred_element_type=F32)
        act = (a * jax.nn.sigmoid(a) * b).astype(BF16)
        y = y + jnp.dot(act, wd_ref[cs, :], preferred_element_type=F32)
    if final_norm:
        y = _rms(y) * gf_ref[...]
    y_ref[...] = y


def _ffn(x2d, gain, w_gate, w_up, w_down, gain_final, final_norm):
    t = x2d.shape[0]
    tm = TOKEN_TILE
    tok = pl.BlockSpec((tm, D_MODEL), lambda i: (i, 0))
    return pl.pallas_call(
        functools.partial(_ffn_kernel, final_norm=final_norm),
        grid=(t // tm,),
        in_specs=[tok, _const_spec((1, D_MODEL)), _const_spec((D_MODEL, D_FF)),
                  _const_spec((D_MODEL, D_FF)), _const_spec((D_FF, D_MODEL)),
                  _const_spec((1, D_MODEL))],
        out_specs=tok,
        out_shape=jax.ShapeDtypeStruct((t, D_MODEL), F32),
        compiler_params=_params("parallel"),
        name="ffn",
    )(x2d, gain, w_gate, w_up, w_down, gain_final)


def _prepare_layer(l, norm_mix, w_in, na_rpb, qk_norm, diff_lambda, diff_subln, w_branch, w_out,
                   norm_ffn, w_ffn_gate, w_ffn_up, w_ffn_down):
    row = lambda g: g.astype(F32).reshape(1, -1)
    return dict(
        lam_init=0.8 - 0.6 * math.exp(-0.3 * l),
        norm_mix=row(norm_mix[l]),
        w_qkv=_permute_columns(w_in[l][:, :QKV_COLS].astype(BF16)),
        w_gate=w_in[l][:, QKV_COLS:].astype(BF16),
        na_bias=_na_bias_table(na_rpb[l]),
        gq=_gqa_gain_tile(qk_norm[l, 0]),
        gk=_gqa_gain_tile(qk_norm[l, 1]),
        diff_lambda=diff_lambda[l].astype(F32),
        diff_subln=row(diff_subln[l]),
        w_branch=w_branch[l].astype(BF16),
        w_out=w_out[l].astype(BF16),
        norm_ffn=row(norm_ffn[l]),
        w_ffn_gate=w_ffn_gate[l].astype(BF16),
        w_ffn_up=w_ffn_up[l].astype(BF16),
        w_ffn_down=w_ffn_down[l].astype(BF16),
    )


def _trunk(x, layers, norm_final):
    batch, n_seq, _ = x.shape
    x2d = x.reshape(batch * n_seq, D_MODEL)
    tables = _rope_tables(n_seq)
    gf = norm_final.astype(F32).reshape(1, -1)
    for l, p in enumerate(layers):
        qa, ka, va, qb, kb, vb, qc, kc, vc = _inproj(x2d, n_seq, p["norm_mix"], p["w_qkv"], tables,
                                                     p["gq"], p["gk"])
        o_a = _na_attention(qa, ka, va, p["na_bias"], batch, n_seq)
        o_b = _gqa_attention(qb, kb, vb, batch, n_seq)
        o_c = _diff_attention(qc, kc, vc, p["diff_lambda"], p["diff_subln"], p["lam_init"], batch, n_seq)
        x2d = _merge(x2d, p["norm_mix"], o_a, o_b, o_c, p["w_gate"], p["w_branch"], p["w_out"])
        x2d = _ffn(x2d, p["norm_ffn"], p["w_ffn_gate"], p["w_ffn_up"], p["w_ffn_down"], gf,
                   final_norm=(l == len(layers) - 1))
    return x2d.reshape(batch, n_seq, D_MODEL)


def kernel(x_prompt, x_sample, norm_mix, w_in, na_rpb, qk_norm, diff_lambda, diff_subln, w_branch,
           w_out, norm_ffn, w_ffn_gate, w_ffn_up, w_ffn_down, norm_final):
    depth = w_in.shape[0]
    layers = [_prepare_layer(l, norm_mix, w_in, na_rpb, qk_norm, diff_lambda, diff_subln, w_branch,
                             w_out, norm_ffn, w_ffn_gate, w_ffn_up, w_ffn_down) for l in range(depth)]
    return (_trunk(x_prompt, layers, norm_final), _trunk(x_sample, layers, norm_final))
```

```python
import functools
import math

import jax
import jax.numpy as jnp
import numpy as np
from jax import lax
from jax.experimental import pallas as pl
from jax.experimental.pallas import tpu as pltpu

F32 = jnp.float32
BF16 = jnp.bfloat16

D_MODEL = 1024
GRID_W = 64
HEAD_DIM = 64
NA_HEADS = 8
NA_MAX_KH = 8
NA_KW = 16
NA_BIAS_ROWS = 2 * NA_MAX_KH - 1
GQA_Q_HEADS = 8
GQA_KV_HEADS = 2
DIFF_HEADS = 4
DIFF_VDIM = 128
BRANCH_WIDTH = 512
ROPE_THETA = 10000.0
RMS_EPS = 1e-6
D_FF = 2816
BF16_SUBLANES = 16
GQA_VT_ROWS = HEAD_DIM + BF16_SUBLANES
DIFF_VT_ROWS = DIFF_VDIM + BF16_SUBLANES
QKV_COLS = 3840

LANES = 128
VMEM_LIMIT_BYTES = 56 * 1024 * 1024

LOG2E = 1.4426950408889634
Q_SCALE = HEAD_DIM ** -0.5 * LOG2E
NEG_BIG = -1e30

TOKEN_TILE = 512
NA_ROWS_PER_STEP = 32
GQA_Q_TILE = 512
DIFF_Q_TILE = 512
KEY_TILE = 256
CHUNKS_PER_ITER = 4
SAFE_EXCESS = 64.0
PROBE_KEYS = 128


def _const_spec(shape):
    nd = len(shape)
    return pl.BlockSpec(shape, lambda *_: (0,) * nd, pipeline_mode=pl.Buffered(1))


def _params(*sem):
    return pltpu.CompilerParams(dimension_semantics=sem, vmem_limit_bytes=VMEM_LIMIT_BYTES)


def _rms(x, eps=RMS_EPS):
    return x * lax.rsqrt(jnp.mean(x * x, axis=-1, keepdims=True) + eps)


def _permute_columns(w):
    d = w.shape[0]
    na = w[:, 0:1536]
    qb = w[:, 1536:2048].reshape(d, 2, 4, 2, 2, 16).transpose(0, 2, 4, 1, 3, 5).reshape(d, 512)
    kb = w[:, 2048:2176].reshape(d, 2, 2, 2, 16).transpose(0, 3, 1, 2, 4).reshape(d, 128)
    vb = w[:, 2176:2304]
    qc, kc = (w[:, c:c + 512].reshape(d, 4, 2, 2, 32).transpose(0, 1, 3, 2, 4).reshape(d, 512)
              for c in (2304, 2816))
    vc = w[:, 3328:3840]
    return jnp.concatenate([na, qb, kb, vb, qc, kc, vc], axis=1)


_C_QA, _C_KA, _C_VA, _C_QB, _C_KB, _C_VB, _C_QC, _C_KC, _C_VC, _C_END = (
    0, 512, 1024, 1536, 2048, 2176, 2304, 2816, 3328, 3840)


def _gqa_gain_tile(g):
    idx = np.zeros((1, LANES), np.int32)
    for x12 in range(2):
        for e in range(2):
            for ab in range(2):
                for i in range(16):
                    idx[0, x12 * 64 + e * 32 + ab * 16 + i] = ab * 32 + x12 * 16 + i
    return g.astype(F32)[idx]


def _rope_tables(n):
    t = jnp.arange(n)
    inv32 = 1.0 / (ROPE_THETA ** (jnp.arange(0, 32, 2, dtype=F32) / 32))
    inv64 = 1.0 / (ROPE_THETA ** (jnp.arange(0, 64, 2, dtype=F32) / 64))
    ang_r = (t // GRID_W).astype(F32)[:, None] * inv32[None, :]
    ang_c = (t % GRID_W).astype(F32)[:, None] * inv32[None, :]
    ang_g = jnp.tile(jnp.concatenate([ang_r, ang_c], axis=-1), (1, 4))
    ang_d = jnp.tile(t.astype(F32)[:, None] * inv64[None, :], (1, 4))
    sign = jnp.where(jnp.arange(LANES) < LANES // 2, -1.0, 1.0).astype(F32)[None, :]
    return jnp.cos(ang_g), jnp.sin(ang_g) * sign, jnp.cos(ang_d), jnp.sin(ang_d) * sign


def _inproj_kernel(x_ref, g_ref, w_ref, cg_ref, sg_ref, cd_ref, sd_ref, gq_ref, gk_ref,
                   qa_ref, ka_ref, va_ref, qb_ref, kb_ref, vb_ref, qc_ref, kc_ref, vc_ref):
    h = (_rms(x_ref[...]) * g_ref[...]).astype(BF16)

    def proj(c0, c1):
        return jnp.dot(h, w_ref[:, c0:c1], preferred_element_type=F32)

    pair = (lax.broadcasted_iota(jnp.int32, (1, LANES), 1) % (LANES // 2)) // 32

    def rope(x, c, s):
        return x * c + pltpu.roll(x, LANES // 2, 1) * s

    def tiles(y):
        return [y[:, c:c + LANES] for c in range(0, y.shape[1], LANES)]

    cg, sg = cg_ref[...], sg_ref[...]

    def gqa(y, gains_ref, out_ref, scale):
        for t, x in enumerate(tiles(y)):
            ssq = x * x
            r = jnp.zeros_like(x)
            for e in range(2):
                tot = jnp.sum(jnp.where(pair == e, ssq, 0.0), axis=-1, keepdims=True)
                r = jnp.where(pair == e, lax.rsqrt(tot * (1.0 / HEAD_DIM) + RMS_EPS), r)
            out = rope(x * r * gains_ref[...], cg, sg) * scale
            out_ref[:, t * LANES:(t + 1) * LANES] = out.astype(BF16)

    gqa(proj(_C_QB, _C_KB), gq_ref, qb_ref, Q_SCALE)
    gqa(proj(_C_KB, _C_VB), gk_ref, kb_ref, 1.0)

    cd, sd = cd_ref[...], sd_ref[...]

    def diff(y, out_ref, scale):
        for t, x in enumerate(tiles(y)):
            out_ref[:, t * LANES:(t + 1) * LANES] = (rope(x, cd, sd) * scale).astype(BF16)

    diff(proj(_C_QC, _C_KC), qc_ref, Q_SCALE)
    diff(proj(_C_KC, _C_VC), kc_ref, 1.0)

    vt = proj(_C_VB, _C_QC).T
    ones = jnp.ones((GQA_VT_ROWS - HEAD_DIM, vt.shape[1]), F32)
    vb_ref[0] = jnp.concatenate([vt[0:HEAD_DIM], ones, vt[HEAD_DIM:], ones], axis=0).astype(BF16)
    vct = proj(_C_VC, _C_END).T
    ones = jnp.ones((DIFF_VT_ROWS - DIFF_VDIM, vct.shape[1]), F32)
    vc_ref[0] = jnp.concatenate(
        [piece for h in range(DIFF_HEADS) for piece in (vct[h * DIFF_VDIM:(h + 1) * DIFF_VDIM], ones)],
        axis=0).astype(BF16)

    qa_ref[...] = (proj(_C_QA, _C_KA) * Q_SCALE).astype(BF16)
    ka_ref[...] = proj(_C_KA, _C_VA).astype(BF16)
    va_ref[...] = proj(_C_VA, _C_QB).astype(BF16)


def _inproj(x2d, n_seq, gain, w_perm, tables, gq, gk):
    t = x2d.shape[0]
    tm = TOKEN_TILE
    assert t % tm == 0 and n_seq % tm == 0
    seq_tiles = n_seq // tm
    tok = lambda c: pl.BlockSpec((tm, c), lambda i: (i, 0))
    tab = pl.BlockSpec((tm, LANES), lambda i: (i % seq_tiles, 0))
    widths = (512, 512, 512, 512, LANES, -GQA_KV_HEADS * GQA_VT_ROWS, 512, 512,
              -DIFF_HEADS * DIFF_VT_ROWS)
    spec = lambda c: tok(c) if c > 0 else pl.BlockSpec((1, -c, tm), lambda i: (i, 0, 0))
    shape = lambda c: jax.ShapeDtypeStruct((t, c) if c > 0 else (t // tm, -c, tm), BF16)
    return pl.pallas_call(
        _inproj_kernel,
        grid=(t // tm,),
        in_specs=[tok(D_MODEL), _const_spec((1, D_MODEL)), _const_spec((D_MODEL, _C_END)),
                  tab, tab, tab, tab, _const_spec((1, LANES)), _const_spec((1, LANES))],
        out_specs=[spec(c) for c in widths],
        out_shape=[shape(c) for c in widths],
        compiler_params=_params("parallel"),
        name="inproj",
    )(x2d, gain, w_perm, *tables, gq, gk)


def _na_bias_table(rpb):
    qc = np.arange(GRID_W)[:, None]
    kc = np.arange(GRID_W)[None, :]
    w_start = np.clip(qc - NA_KW // 2, 0, GRID_W - NA_KW)
    valid = (kc >= w_start) & (kc < w_start + NA_KW)
    dc = np.clip(kc - qc, -(NA_KW - 1), NA_KW - 1) + NA_KW - 1
    b = rpb.astype(F32)[:, :, dc]
    b = jnp.where(valid[None, None], b * LOG2E, NEG_BIG)
    b = b.reshape(NA_HEADS // 2, 2, NA_BIAS_ROWS, GRID_W, GRID_W)
    b = b.transpose(0, 2, 4, 1, 3)
    return b.reshape(NA_HEADS // 2, NA_BIAS_ROWS * GRID_W, 2 * GRID_W)


def _na_kernel(q_ref, k_ref, v_ref, b_ref, o_ref, *, rows):
    c = pl.program_id(2)
    lo = lax.broadcasted_iota(jnp.int32, (1, LANES), 1) < HEAD_DIM
    win_tokens = NA_MAX_KH * GRID_W
    rows_per_step = q_ref.shape[0] // GRID_W

    def window(i):
        r = c * rows_per_step + i
        r_start = jnp.clip(r - NA_MAX_KH // 2, 0, rows - NA_MAX_KH)
        return r - r_start, pl.multiple_of(r_start * GRID_W, GRID_W)

    def scores(i):
        _, t0 = window(i)
        q = q_ref[i * GRID_W:(i + 1) * GRID_W, :]
        zero = jnp.zeros_like(q)
        qq = jnp.concatenate([jnp.where(lo, q, zero), jnp.where(lo, zero, q)], axis=0)
        return _nt_dot(k_ref[pl.ds(t0, win_tokens), :], qq)

    ones = jnp.ones((win_tokens, LANES), BF16)

    def attend(i, st):
        off, t0 = window(i)
        b0 = pl.multiple_of((NA_MAX_KH - 1 - off) * GRID_W, GRID_W)
        st = st + b_ref[0, pl.ds(b0, win_tokens), :]
        pt = jnp.exp2(st - jnp.max(st, axis=0, keepdims=True)).T.astype(BF16)
        v1 = jnp.concatenate([v_ref[pl.ds(t0, win_tokens), :], ones], axis=1)
        o2 = jnp.dot(pt, v1, preferred_element_type=F32)
        o2 = o2[:, 0:LANES] * (1.0 / o2[:, LANES:])
        o_ref[i * GRID_W:(i + 1) * GRID_W, :] = jnp.where(lo, o2[0:GRID_W], o2[GRID_W:]).astype(BF16)

    _skewed(list(range(rows_per_step)), scores, attend)


def _na_attention(q, k, v, bias, batch, n_seq):
    rows = n_seq // GRID_W
    assert rows >= NA_MAX_KH
    chunk = math.gcd(rows, NA_ROWS_PER_STEP) * GRID_W
    chunks = n_seq // chunk
    qspec = pl.BlockSpec((chunk, LANES), lambda b, hp, c: (b * chunks + c, hp))
    kvspec = pl.BlockSpec((n_seq, LANES), lambda b, hp, c: (b, hp))
    bspec = pl.BlockSpec((1, NA_BIAS_ROWS * GRID_W, 2 * GRID_W), lambda b, hp, c: (hp, 0, 0))
    return pl.pallas_call(
        functools.partial(_na_kernel, rows=rows),
        grid=(batch, NA_HEADS // 2, chunks),
        in_specs=[qspec, kvspec, kvspec, bspec],
        out_specs=qspec,
        out_shape=jax.ShapeDtypeStruct(q.shape, BF16),
        compiler_params=_params("parallel", "parallel", "arbitrary"),
        name="na_attn",
    )(q, k, v, bias)


def _skewed(items, first, second):
    pending = first(items[0])
    for i, item in enumerate(items):
        upcoming = first(items[i + 1]) if i + 1 < len(items) else None
        second(item, pending)
        pending = upcoming


def _nt_dot(a, b):
    return lax.dot_general(a, b, (((1,), (1,)), ((), ())), preferred_element_type=F32)


def _exp2_bf16(x):
    return jnp.exp2(x.astype(BF16))


def _streaming_softmax_matmul(n_chunks, heads, scores, values_t, m_sc, d_sc, acc_sc, write_out):
    acc_sc[...] = jnp.zeros_like(acc_sc)
    d_sc[...] = jnp.full_like(d_sc, -jnp.inf)
    per_iter = math.gcd(n_chunks, CHUNKS_PER_ITER)

    def step(it, carry):
        def first(item):
            c, h, s = item
            return scores(it * per_iter + c, h, s * KEY_TILE, KEY_TILE)

        def second(item, st):
            c, h, s = item
            d_sc[h] = jnp.maximum(d_sc[h], jnp.max(st.reshape(-1, 8, st.shape[-1]), axis=0))
            acc_sc[h] += jnp.dot(values_t(it * per_iter + c, h, s * KEY_TILE, KEY_TILE),
                                 _exp2_bf16(st - m_sc[h]), preferred_element_type=F32)

        _skewed([(c, h, s) for c in range(per_iter) for h in range(heads)
                 for s in range(TOKEN_TILE // KEY_TILE)], first, second)
        return carry

    lax.fori_loop(0, n_chunks // per_iter, step, 0)
    write_out()

    excess = None
    for h in range(heads):
        e = jnp.max(d_sc[h], axis=0, keepdims=True) - m_sc[h]
        excess = e if excess is None else jnp.maximum(excess, e)

    @pl.when(jnp.max(excess) > SAFE_EXCESS)
    def _():
        acc_sc[...] = jnp.zeros_like(acc_sc)
        m_sc[...] = jnp.full_like(m_sc, -jnp.inf)

        def chunk(c, carry):
            for h in range(heads):
                st = scores(c, h)
                m_prev = m_sc[h]
                m_new = jnp.maximum(m_prev, jnp.max(st, axis=0, keepdims=True))
                acc_sc[h] = (jnp.exp2(m_prev - m_new) * acc_sc[h]
                             + jnp.dot(values_t(c, h), _exp2_bf16(st - m_new),
                                       preferred_element_type=F32))
                m_sc[h] = m_new
            return carry

        lax.fori_loop(0, n_chunks, chunk, 0)
        write_out()


def _tile_cols(t):
    return slice(t * LANES, (t + 1) * LANES)


def _prepare_queries(q_ref, k_ref, qm_sc, m_sc, layout):
    qt = q_ref[...].astype(F32).T
    pair = (lax.broadcasted_iota(jnp.int32, (LANES, 1), 0) % (LANES // 2)) // 32
    for h in range(qm_sc.shape[0]):
        q_tile, e, k_tile = layout(h)
        qm_sc[h] = jnp.where(pair == e, qt[_tile_cols(q_tile)], 0.0).astype(BF16)
        probe = jnp.dot(k_ref[0:PROBE_KEYS, _tile_cols(k_tile)], qm_sc[h], preferred_element_type=F32)
        m_sc[h] = jnp.max(probe, axis=0, keepdims=True)


def _chunk_scores(k_ref, qm_sc, layout, c, h, k0=0, kn=TOKEN_TILE):
    t0 = pl.multiple_of(c * TOKEN_TILE + k0, KEY_TILE)
    return jnp.dot(k_ref[pl.ds(t0, kn), _tile_cols(layout(h)[2])], qm_sc[h],
                   preferred_element_type=F32)


def _gqa_layout(h):
    return h % 4, h // 4, 0


def _diff_layout(j):
    return j // 2, j % 2, j // 2


def _gqa_kernel(q_ref, k_ref, vt_ref, o_ref, qm_sc, m_sc, d_sc, acc_sc):
    _prepare_queries(q_ref, k_ref, qm_sc, m_sc, _gqa_layout)
    scores = functools.partial(_chunk_scores, k_ref, qm_sc, _gqa_layout)

    def values_t(c, h, k0=0, kn=TOKEN_TILE):
        return vt_ref[c, (h // 4) * GQA_VT_ROWS:(h // 4 + 1) * GQA_VT_ROWS, k0:k0 + kn]

    def write_out():
        for hp in range(GQA_Q_HEADS // 2):
            halves = []
            for h in (2 * hp, 2 * hp + 1):
                acc = acc_sc[h]
                halves.append(acc[0:HEAD_DIM] * (1.0 / acc[HEAD_DIM:HEAD_DIM + 1]))
            o_ref[:, hp * LANES:(hp + 1) * LANES] = jnp.concatenate(halves, axis=0).T.astype(BF16)

    _streaming_softmax_matmul(vt_ref.shape[0], GQA_Q_HEADS, scores, values_t, m_sc, d_sc, acc_sc,
                              write_out)


def _gqa_attention(q, k, vt, batch, n_seq):
    tq = GQA_Q_TILE
    assert n_seq % tq == 0 and n_seq % TOKEN_TILE == 0
    nq, n_chunks = n_seq // tq, n_seq // TOKEN_TILE
    return pl.pallas_call(
        _gqa_kernel,
        grid=(batch, nq),
        in_specs=[pl.BlockSpec((tq, BRANCH_WIDTH), lambda b, i: (b * nq + i, 0)),
                  pl.BlockSpec((n_seq, LANES), lambda b, i: (b, 0)),
                  pl.BlockSpec((n_chunks, GQA_KV_HEADS * GQA_VT_ROWS, TOKEN_TILE),
                               lambda b, i: (b, 0, 0))],
        out_specs=pl.BlockSpec((tq, BRANCH_WIDTH), lambda b, i: (b * nq + i, 0)),
        out_shape=jax.ShapeDtypeStruct((batch * n_seq, BRANCH_WIDTH), BF16),
        scratch_shapes=[pltpu.VMEM((GQA_Q_HEADS, LANES, tq), BF16),
                        pltpu.VMEM((GQA_Q_HEADS, 1, tq), F32),
                        pltpu.VMEM((GQA_Q_HEADS, 8, tq), F32),
                        pltpu.VMEM((GQA_Q_HEADS, GQA_VT_ROWS, tq), F32)],
        compiler_params=_params("parallel", "arbitrary"),
        name="gqa_attn",
    )(q, k, vt)


def _diff_kernel(q_ref, k_ref, vt_ref, lam_ref, g_ref, o_ref, qm_sc, m_sc, d_sc, acc_sc, *, lam_init):
    subheads = 2 * DIFF_HEADS
    _prepare_queries(q_ref, k_ref, qm_sc, m_sc, _diff_layout)
    scores = functools.partial(_chunk_scores, k_ref, qm_sc, _diff_layout)

    def values_t(c, j, k0=0, kn=TOKEN_TILE):
        return vt_ref[c, (j // 2) * DIFF_VT_ROWS:(j // 2 + 1) * DIFF_VT_ROWS, k0:k0 + kn]

    lp = lam_ref[...]
    lam = (jnp.exp(jnp.sum(lp[0:1] * lp[1:2], axis=-1, keepdims=True))
           - jnp.exp(jnp.sum(lp[2:3] * lp[3:4], axis=-1, keepdims=True)) + lam_init)

    def write_out():
        for h in range(DIFF_HEADS):
            a1, a2 = acc_sc[2 * h], acc_sc[2 * h + 1]
            ot = (a1[0:DIFF_VDIM] * (1.0 / a1[DIFF_VDIM:DIFF_VDIM + 1])
                  - a2[0:DIFF_VDIM] * (lam / a2[DIFF_VDIM:DIFF_VDIM + 1]))
            o = _rms(ot.T) * g_ref[...] * (1.0 - lam_init)
            o_ref[:, h * DIFF_VDIM:(h + 1) * DIFF_VDIM] = o.astype(BF16)

    _streaming_softmax_matmul(vt_ref.shape[0], subheads, scores, values_t, m_sc, d_sc, acc_sc,
                              write_out)


def _diff_attention(q, k, vt, lam_params, subln_g, lam_init, batch, n_seq):
    tq = DIFF_Q_TILE
    assert n_seq % tq == 0 and n_seq % TOKEN_TILE == 0
    nq, n_chunks = n_seq // tq, n_seq // TOKEN_TILE
    subheads = 2 * DIFF_HEADS
    return pl.pallas_call(
        functools.partial(_diff_kernel, lam_init=lam_init),
        grid=(batch, nq),
        in_specs=[pl.BlockSpec((tq, BRANCH_WIDTH), lambda b, i: (b * nq + i, 0)),
                  pl.BlockSpec((n_seq, BRANCH_WIDTH), lambda b, i: (b, 0)),
                  pl.BlockSpec((n_chunks, DIFF_HEADS * DIFF_VT_ROWS, TOKEN_TILE),
                               lambda b, i: (b, 0, 0)),
                  _const_spec((4, HEAD_DIM)), _const_spec((1, DIFF_VDIM))],
        out_specs=pl.BlockSpec((tq, BRANCH_WIDTH), lambda b, i: (b * nq + i, 0)),
        out_shape=jax.ShapeDtypeStruct((batch * n_seq, BRANCH_WIDTH), BF16),
        scratch_shapes=[pltpu.VMEM((subheads, LANES, tq), BF16),
                        pltpu.VMEM((subheads, 1, tq), F32),
                        pltpu.VMEM((subheads, 8, tq), F32),
                        pltpu.VMEM((subheads, DIFF_VT_ROWS, tq), F32)],
        compiler_params=_params("parallel", "arbitrary"),
        name="diff_attn",
    )(q, k, vt, lam_params, subln_g)


def _merge_kernel(x_ref, g_ref, oa_ref, ob_ref, oc_ref, wg_ref, wb_ref, wo_ref, y_ref):
    x = x_ref[...]
    h = (_rms(x) * g_ref[...]).astype(BF16)
    merged = None
    for i, o_ref in enumerate((oa_ref, ob_ref, oc_ref)):
        gate = jax.nn.sigmoid(jnp.dot(h, wg_ref[:, i * D_MODEL:(i + 1) * D_MODEL],
                                      preferred_element_type=F32))
        term = gate * jnp.dot(o_ref[...], wb_ref[i], preferred_element_type=F32)
        merged = term if merged is None else merged + term
    y_ref[...] = x + jnp.dot(merged.astype(BF16), wo_ref[...], preferred_element_type=F32)


def _merge(x2d, gain, oa, ob, oc, w_gate, w_branch, w_out):
    t = x2d.shape[0]
    tm = TOKEN_TILE
    tok = lambda c: pl.BlockSpec((tm, c), lambda i: (i, 0))
    return pl.pallas_call(
        _merge_kernel,
        grid=(t // tm,),
        in_specs=[tok(D_MODEL), _const_spec((1, D_MODEL)), tok(BRANCH_WIDTH), tok(BRANCH_WIDTH),
                  tok(BRANCH_WIDTH), _const_spec((D_MODEL, 3 * D_MODEL)),
                  _const_spec((3, BRANCH_WIDTH, D_MODEL)), _const_spec((D_MODEL, D_MODEL))],
        out_specs=tok(D_MODEL),
        out_shape=jax.ShapeDtypeStruct((t, D_MODEL), F32),
        compiler_params=_params("parallel"),
        name="merge",
    )(x2d, gain, oa, ob, oc, w_gate, w_branch, w_out)


FFN_CHUNKS = 2


def _ffn_kernel(x_ref, g_ref, wg_ref, wu_ref, wd_ref, gf_ref, y_ref, *, final_norm):
    x = x_ref[...]
    h = (_rms(x) * g_ref[...]).astype(BF16)
    width = D_FF // FFN_CHUNKS
    y = x
    for c in range(FFN_CHUNKS):
        cs = slice(c * width, (c + 1) * width)
        a = jnp.dot(h, wg_ref[:, cs], preferred_element_type=F32)
        b = jnp.dot(h, wu_ref[:, cs], preferred_element_type=F32)
        act = (a * jax.nn.sigmoid(a) * b).astype(BF16)
        y = y + jnp.dot(act, wd_ref[cs, :], preferred_element_type=F32)
    if final_norm:
        y = _rms(y) * gf_ref[...]
    y_ref[...] = y


def _ffn(x2d, gain, w_gate, w_up, w_down, gain_final, final_norm):
    t = x2d.shape[0]
    tm = TOKEN_TILE
    tok = pl.BlockSpec((tm, D_MODEL), lambda i: (i, 0))
    return pl.pallas_call(
        functools.partial(_ffn_kernel, final_norm=final_norm),
        grid=(t // tm,),
        in_specs=[tok, _const_spec((1, D_MODEL)), _const_spec((D_MODEL, D_FF)),
                  _const_spec((D_MODEL, D_FF)), _const_spec((D_FF, D_MODEL)),
                  _const_spec((1, D_MODEL))],
        out_specs=tok,
        out_shape=jax.ShapeDtypeStruct((t, D_MODEL), F32),
        compiler_params=_params("parallel"),
        name="ffn",
    )(x2d, gain, w_gate, w_up, w_down, gain_final)


def _prepare_layer(l, norm_mix, w_in, na_rpb, qk_norm, diff_lambda, diff_subln, w_branch, w_out,
                   norm_ffn, w_ffn_gate, w_ffn_up, w_ffn_down):
    row = lambda g: g.astype(F32).reshape(1, -1)
    return dict(
        lam_init=0.8 - 0.6 * math.exp(-0.3 * l),
        norm_mix=row(norm_mix[l]),
        w_qkv=_permute_columns(w_in[l][:, :QKV_COLS].astype(BF16)),
        w_gate=w_in[l][:, QKV_COLS:].astype(BF16),
        na_bias=_na_bias_table(na_rpb[l]),
        gq=_gqa_gain_tile(qk_norm[l, 0]),
        gk=_gqa_gain_tile(qk_norm[l, 1]),
        diff_lambda=diff_lambda[l].astype(F32),
        diff_subln=row(diff_subln[l]),
        w_branch=w_branch[l].astype(BF16),
        w_out=w_out[l].astype(BF16),
        norm_ffn=row(norm_ffn[l]),
        w_ffn_gate=w_ffn_gate[l].astype(BF16),
        w_ffn_up=w_ffn_up[l].astype(BF16),
        w_ffn_down=w_ffn_down[l].astype(BF16),
    )


def _trunk(x, layers, norm_final):
    batch, n_seq, _ = x.shape
    x2d = x.reshape(batch * n_seq, D_MODEL)
    tables = _rope_tables(n_seq)
    gf = norm_final.astype(F32).reshape(1, -1)
    for l, p in enumerate(layers):
        qa, ka, va, qb, kb, vb, qc, kc, vc = _inproj(x2d, n_seq, p["norm_mix"], p["w_qkv"], tables,
                                                     p["gq"], p["gk"])
        o_a = _na_attention(qa, ka, va, p["na_bias"], batch, n_seq)
        o_b = _gqa_attention(qb, kb, vb, batch, n_seq)
        o_c = _diff_attention(qc, kc, vc, p["diff_lambda"], p["diff_subln"], p["lam_init"], batch, n_seq)
        x2d = _merge(x2d, p["norm_mix"], o_a, o_b, o_c, p["w_gate"], p["w_branch"], p["w_out"])
        x2d = _ffn(x2d, p["norm_ffn"], p["w_ffn_gate"], p["w_ffn_up"], p["w_ffn_down"], gf,
                   final_norm=(l == len(layers) - 1))
    return x2d.reshape(batch, n_seq, D_MODEL)


def kernel(x_prompt, x_sample, norm_mix, w_in, na_rpb, qk_norm, diff_lambda, diff_subln, w_branch,
           w_out, norm_ffn, w_ffn_gate, w_ffn_up, w_ffn_down, norm_final):
    depth = w_in.shape[0]
    layers = [_prepare_layer(l, norm_mix, w_in, na_rpb, qk_norm, diff_lambda, diff_subln, w_branch,
                             w_out, norm_ffn, w_ffn_gate, w_ffn_up, w_ffn_down) for l in range(depth)]
    return (_trunk(x_prompt, layers, norm_final), _trunk(x_sample, layers, norm_final))
```

```python
import functools
import math

import jax
import jax.numpy as jnp
import numpy as np
from jax import lax
from jax.experimental import pallas as pl
from jax.experimental.pallas import tpu as pltpu

F32 = jnp.float32
BF16 = jnp.bfloat16

D_MODEL = 1024
GRID_W = 64
HEAD_DIM = 64
NA_HEADS = 8
NA_MAX_KH = 8
NA_KW = 16
NA_BIAS_ROWS = 2 * NA_MAX_KH - 1
GQA_Q_HEADS = 8
GQA_KV_HEADS = 2
DIFF_HEADS = 4
DIFF_VDIM = 128
BRANCH_WIDTH = 512
ROPE_THETA = 10000.0
RMS_EPS = 1e-6
D_FF = 2816
BF16_SUBLANES = 16
GQA_VT_ROWS = HEAD_DIM + BF16_SUBLANES
DIFF_VT_ROWS = DIFF_VDIM + BF16_SUBLANES
QKV_COLS = 3840

LANES = 128
VMEM_LIMIT_BYTES = 56 * 1024 * 1024

LOG2E = 1.4426950408889634
Q_SCALE = HEAD_DIM ** -0.5 * LOG2E
NEG_BIG = -1e30

TOKEN_TILE = 512
NA_ROWS_PER_STEP = 32
GQA_Q_TILE = 512
DIFF_Q_TILE = 512
CHUNKS_PER_ITER = 4
SAFE_EXCESS = 64.0
PROBE_KEYS = 128


def _const_spec(shape):
    nd = len(shape)
    return pl.BlockSpec(shape, lambda *_: (0,) * nd, pipeline_mode=pl.Buffered(1))


def _params(*sem):
    return pltpu.CompilerParams(dimension_semantics=sem, vmem_limit_bytes=VMEM_LIMIT_BYTES)


def _rms(x, eps=RMS_EPS):
    return x * lax.rsqrt(jnp.mean(x * x, axis=-1, keepdims=True) + eps)


def _permute_columns(w):
    d = w.shape[0]
    na = w[:, 0:1536]
    qb = w[:, 1536:2048].reshape(d, 2, 4, 2, 2, 16).transpose(0, 2, 4, 1, 3, 5).reshape(d, 512)
    kb = w[:, 2048:2176].reshape(d, 2, 2, 2, 16).transpose(0, 3, 1, 2, 4).reshape(d, 128)
    vb = w[:, 2176:2304]
    qc, kc = (w[:, c:c + 512].reshape(d, 4, 2, 2, 32).transpose(0, 1, 3, 2, 4).reshape(d, 512)
              for c in (2304, 2816))
    vc = w[:, 3328:3840]
    return jnp.concatenate([na, qb, kb, vb, qc, kc, vc], axis=1)


_C_QA, _C_KA, _C_VA, _C_QB, _C_KB, _C_VB, _C_QC, _C_KC, _C_VC, _C_END = (
    0, 512, 1024, 1536, 2048, 2176, 2304, 2816, 3328, 3840)


def _gqa_gain_tile(g):
    idx = np.zeros((1, LANES), np.int32)
    for x12 in range(2):
        for e in range(2):
            for ab in range(2):
                for i in range(16):
                    idx[0, x12 * 64 + e * 32 + ab * 16 + i] = ab * 32 + x12 * 16 + i
    return g.astype(F32)[idx]


def _rope_tables(n):
    t = jnp.arange(n)
    inv32 = 1.0 / (ROPE_THETA ** (jnp.arange(0, 32, 2, dtype=F32) / 32))
    inv64 = 1.0 / (ROPE_THETA ** (jnp.arange(0, 64, 2, dtype=F32) / 64))
    ang_r = (t // GRID_W).astype(F32)[:, None] * inv32[None, :]
    ang_c = (t % GRID_W).astype(F32)[:, None] * inv32[None, :]
    ang_g = jnp.tile(jnp.concatenate([ang_r, ang_c], axis=-1), (1, 4))
    ang_d = jnp.tile(t.astype(F32)[:, None] * inv64[None, :], (1, 4))
    sign = jnp.where(jnp.arange(LANES) < LANES // 2, -1.0, 1.0).astype(F32)[None, :]
    return jnp.cos(ang_g), jnp.sin(ang_g) * sign, jnp.cos(ang_d), jnp.sin(ang_d) * sign


def _inproj_kernel(x_ref, g_ref, w_ref, cg_ref, sg_ref, cd_ref, sd_ref, gq_ref, gk_ref,
                   qa_ref, ka_ref, va_ref, qb_ref, kb_ref, vb_ref, qc_ref, kc_ref, vc_ref):
    h = (_rms(x_ref[...]) * g_ref[...]).astype(BF16)

    def proj(c0, c1):
        return jnp.dot(h, w_ref[:, c0:c1], preferred_element_type=F32)

    pair = (lax.broadcasted_iota(jnp.int32, (1, LANES), 1) % (LANES // 2)) // 32

    def rope(x, c, s):
        return x * c + pltpu.roll(x, LANES // 2, 1) * s

    def tiles(y):
        return [y[:, c:c + LANES] for c in range(0, y.shape[1], LANES)]

    cg, sg = cg_ref[...], sg_ref[...]

    def gqa(y, gains_ref, out_ref, scale):
        for t, x in enumerate(tiles(y)):
            ssq = x * x
            r = jnp.zeros_like(x)
            for e in range(2):
                tot = jnp.sum(jnp.where(pair == e, ssq, 0.0), axis=-1, keepdims=True)
                r = jnp.where(pair == e, lax.rsqrt(tot * (1.0 / HEAD_DIM) + RMS_EPS), r)
            out = rope(x * r * gains_ref[...], cg, sg) * scale
            out_ref[:, t * LANES:(t + 1) * LANES] = out.astype(BF16)

    gqa(proj(_C_QB, _C_KB), gq_ref, qb_ref, Q_SCALE)
    kv = proj(_C_KB, _C_QC)
    gqa(kv[:, 0:LANES], gk_ref, kb_ref, 1.0)

    cd, sd = cd_ref[...], sd_ref[...]

    def diff(y, out_ref, scale):
        for t, x in enumerate(tiles(y)):
            out_ref[:, t * LANES:(t + 1) * LANES] = (rope(x, cd, sd) * scale).astype(BF16)

    diff(proj(_C_QC, _C_KC), qc_ref, Q_SCALE)
    diff(proj(_C_KC, _C_VC), kc_ref, 1.0)

    vt = kv[:, LANES:].T
    ones = jnp.ones((GQA_VT_ROWS - HEAD_DIM, vt.shape[1]), F32)
    vb_ref[0] = jnp.concatenate([vt[0:HEAD_DIM], ones, vt[HEAD_DIM:], ones], axis=0).astype(BF16)
    vct = proj(_C_VC, _C_END).T
    ones = jnp.ones((DIFF_VT_ROWS - DIFF_VDIM, vct.shape[1]), F32)
    vc_ref[0] = jnp.concatenate(
        [piece for h in range(DIFF_HEADS) for piece in (vct[h * DIFF_VDIM:(h + 1) * DIFF_VDIM], ones)],
        axis=0).astype(BF16)

    qa_ref[...] = (proj(_C_QA, _C_KA) * Q_SCALE).astype(BF16)
    ka_ref[...] = proj(_C_KA, _C_VA).astype(BF16)
    va_ref[...] = proj(_C_VA, _C_QB).astype(BF16)


def _inproj(x2d, n_seq, gain, w_perm, tables, gq, gk):
    t = x2d.shape[0]
    tm = TOKEN_TILE
    assert t % tm == 0 and n_seq % tm == 0
    seq_tiles = n_seq // tm
    tok = lambda c: pl.BlockSpec((tm, c), lambda i: (i, 0))
    tab = pl.BlockSpec((tm, LANES), lambda i: (i % seq_tiles, 0))
    widths = (512, 512, 512, 512, LANES, -GQA_KV_HEADS * GQA_VT_ROWS, 512, 512,
              -DIFF_HEADS * DIFF_VT_ROWS)
    spec = lambda c: tok(c) if c > 0 else pl.BlockSpec((1, -c, tm), lambda i: (i, 0, 0))
    shape = lambda c: jax.ShapeDtypeStruct((t, c) if c > 0 else (t // tm, -c, tm), BF16)
    return pl.pallas_call(
        _inproj_kernel,
        grid=(t // tm,),
        in_specs=[tok(D_MODEL), _const_spec((1, D_MODEL)), _const_spec((D_MODEL, _C_END)),
                  tab, tab, tab, tab, _const_spec((1, LANES)), _const_spec((1, LANES))],
        out_specs=[spec(c) for c in widths],
        out_shape=[shape(c) for c in widths],
        compiler_params=_params("parallel"),
        name="inproj",
    )(x2d, gain, w_perm, *tables, gq, gk)


def _na_bias_table(rpb):
    qc = np.arange(GRID_W)[:, None]
    kc = np.arange(GRID_W)[None, :]
    w_start = np.clip(qc - NA_KW // 2, 0, GRID_W - NA_KW)
    valid = (kc >= w_start) & (kc < w_start + NA_KW)
    dc = np.clip(kc - qc, -(NA_KW - 1), NA_KW - 1) + NA_KW - 1
    b = rpb.astype(F32)[:, :, dc]
    b = jnp.where(valid[None, None], b * LOG2E, NEG_BIG)
    b = b.reshape(NA_HEADS // 2, 2, NA_BIAS_ROWS, GRID_W, GRID_W)
    b = b.transpose(0, 2, 4, 1, 3)
    return b.reshape(NA_HEADS // 2, NA_BIAS_ROWS * GRID_W, 2 * GRID_W)


def _na_kernel(q_ref, k_ref, v_ref, b_ref, o_ref, *, rows):
    c = pl.program_id(2)
    lo = lax.broadcasted_iota(jnp.int32, (1, LANES), 1) < HEAD_DIM
    win_tokens = NA_MAX_KH * GRID_W
    rows_per_step = q_ref.shape[0] // GRID_W

    def window(i):
        r = c * rows_per_step + i
        r_start = jnp.clip(r - NA_MAX_KH // 2, 0, rows - NA_MAX_KH)
        return r - r_start, pl.multiple_of(r_start * GRID_W, GRID_W)

    def scores(i):
        _, t0 = window(i)
        q = q_ref[i * GRID_W:(i + 1) * GRID_W, :]
        zero = jnp.zeros_like(q)
        qq = jnp.concatenate([jnp.where(lo, q, zero), jnp.where(lo, zero, q)], axis=0)
        return _nt_dot(k_ref[pl.ds(t0, win_tokens), :], qq)

    ones = jnp.ones((win_tokens, LANES), BF16)

    def attend(i, st):
        off, t0 = window(i)
        b0 = pl.multiple_of((NA_MAX_KH - 1 - off) * GRID_W, GRID_W)
        st = st + b_ref[0, pl.ds(b0, win_tokens), :]
        pt = jnp.exp2(st - jnp.max(st, axis=0, keepdims=True)).T.astype(BF16)
        v1 = jnp.concatenate([v_ref[pl.ds(t0, win_tokens), :], ones], axis=1)
        o2 = jnp.dot(pt, v1, preferred_element_type=F32)
        o2 = o2[:, 0:LANES] * (1.0 / o2[:, LANES:])
        o_ref[i * GRID_W:(i + 1) * GRID_W, :] = jnp.where(lo, o2[0:GRID_W], o2[GRID_W:]).astype(BF16)

    _skewed(list(range(rows_per_step)), scores, attend)


def _na_attention(q, k, v, bias, batch, n_seq):
    rows = n_seq // GRID_W
    assert rows >= NA_MAX_KH
    chunk = math.gcd(rows, NA_ROWS_PER_STEP) * GRID_W
    chunks = n_seq // chunk
    qspec = pl.BlockSpec((chunk, LANES), lambda b, hp, c: (b * chunks + c, hp))
    kvspec = pl.BlockSpec((n_seq, LANES), lambda b, hp, c: (b, hp))
    bspec = pl.BlockSpec((1, NA_BIAS_ROWS * GRID_W, 2 * GRID_W), lambda b, hp, c: (hp, 0, 0))
    return pl.pallas_call(
        functools.partial(_na_kernel, rows=rows),
        grid=(batch, NA_HEADS // 2, chunks),
        in_specs=[qspec, kvspec, kvspec, bspec],
        out_specs=qspec,
        out_shape=jax.ShapeDtypeStruct(q.shape, BF16),
        compiler_params=_params("parallel", "parallel", "arbitrary"),
        name="na_attn",
    )(q, k, v, bias)


def _skewed(items, first, second):
    pending = first(items[0])
    for i, item in enumerate(items):
        upcoming = first(items[i + 1]) if i + 1 < len(items) else None
        second(item, pending)
        pending = upcoming


def _nt_dot(a, b):
    return lax.dot_general(a, b, (((1,), (1,)), ((), ())), preferred_element_type=F32)


def _exp2_bf16(x):
    return jnp.exp2(x.astype(BF16))


def _streaming_softmax_matmul(n_chunks, heads, scores, values_t, m_sc, d_sc, acc_sc, write_out):
    acc_sc[...] = jnp.zeros_like(acc_sc)
    d_sc[...] = jnp.full_like(d_sc, -jnp.inf)
    per_iter = math.gcd(n_chunks, CHUNKS_PER_ITER)

    def step(it, carry):
        def first(item):
            c, h = item
            return scores(it * per_iter + c, h)

        def second(item, st):
            c, h = item
            d_sc[h] = jnp.maximum(d_sc[h], jnp.max(st.reshape(-1, 8, st.shape[-1]), axis=0))
            acc_sc[h] += jnp.dot(values_t(it * per_iter + c, h), _exp2_bf16(st - m_sc[h]),
                                 preferred_element_type=F32)

        _skewed([(c, h) for c in range(per_iter) for h in range(heads)], first, second)
        return carry

    lax.fori_loop(0, n_chunks // per_iter, step, 0)
    write_out()

    excess = None
    for h in range(heads):
        e = jnp.max(d_sc[h], axis=0, keepdims=True) - m_sc[h]
        excess = e if excess is None else jnp.maximum(excess, e)

    @pl.when(jnp.max(excess) > SAFE_EXCESS)
    def _():
        acc_sc[...] = jnp.zeros_like(acc_sc)
        m_sc[...] = jnp.full_like(m_sc, -jnp.inf)

        def chunk(c, carry):
            for h in range(heads):
                st = scores(c, h)
                m_prev = m_sc[h]
                m_new = jnp.maximum(m_prev, jnp.max(st, axis=0, keepdims=True))
                acc_sc[h] = (jnp.exp2(m_prev - m_new) * acc_sc[h]
                             + jnp.dot(values_t(c, h), _exp2_bf16(st - m_new),
                                       preferred_element_type=F32))
                m_sc[h] = m_new
            return carry

        lax.fori_loop(0, n_chunks, chunk, 0)
        write_out()


def _tile_cols(t):
    return slice(t * LANES, (t + 1) * LANES)


def _prepare_queries(q_ref, k_ref, qm_sc, m_sc, layout):
    qt = q_ref[...].astype(F32).T
    pair = (lax.broadcasted_iota(jnp.int32, (LANES, 1), 0) % (LANES // 2)) // 32
    for h in range(qm_sc.shape[0]):
        q_tile, e, k_tile = layout(h)
        qm_sc[h] = jnp.where(pair == e, qt[_tile_cols(q_tile)], 0.0).astype(BF16)
        probe = jnp.dot(k_ref[0:PROBE_KEYS, _tile_cols(k_tile)], qm_sc[h], preferred_element_type=F32)
        m_sc[h] = jnp.max(probe, axis=0, keepdims=True)


def _chunk_scores(k_ref, qm_sc, layout, c, h):
    t0 = pl.multiple_of(c * TOKEN_TILE, TOKEN_TILE)
    return jnp.dot(k_ref[pl.ds(t0, TOKEN_TILE), _tile_cols(layout(h)[2])], qm_sc[h],
                   preferred_element_type=F32)


def _gqa_layout(h):
    return h % 4, h // 4, 0


def _diff_layout(j):
    return j // 2, j % 2, j // 2


def _gqa_kernel(q_ref, k_ref, vt_ref, o_ref, qm_sc, m_sc, d_sc, acc_sc):
    _prepare_queries(q_ref, k_ref, qm_sc, m_sc, _gqa_layout)
    scores = functools.partial(_chunk_scores, k_ref, qm_sc, _gqa_layout)

    def values_t(c, h):
        return vt_ref[c, (h // 4) * GQA_VT_ROWS:(h // 4 + 1) * GQA_VT_ROWS, :]

    def write_out():
        for hp in range(GQA_Q_HEADS // 2):
            halves = []
            for h in (2 * hp, 2 * hp + 1):
                acc = acc_sc[h]
                halves.append(acc[0:HEAD_DIM] * (1.0 / acc[HEAD_DIM:HEAD_DIM + 1]))
            o_ref[:, hp * LANES:(hp + 1) * LANES] = jnp.concatenate(halves, axis=0).T.astype(BF16)

    _streaming_softmax_matmul(vt_ref.shape[0], GQA_Q_HEADS, scores, values_t, m_sc, d_sc, acc_sc,
                              write_out)


def _gqa_attention(q, k, vt, batch, n_seq):
    tq = GQA_Q_TILE
    assert n_seq % tq == 0 and n_seq % TOKEN_TILE == 0
    nq, n_chunks = n_seq // tq, n_seq // TOKEN_TILE
    return pl.pallas_call(
        _gqa_kernel,
        grid=(batch, nq),
        in_specs=[pl.BlockSpec((tq, BRANCH_WIDTH), lambda b, i: (b * nq + i, 0)),
                  pl.BlockSpec((n_seq, LANES), lambda b, i: (b, 0)),
                  pl.BlockSpec((n_chunks, GQA_KV_HEADS * GQA_VT_ROWS, TOKEN_TILE),
                               lambda b, i: (b, 0, 0))],
        out_specs=pl.BlockSpec((tq, BRANCH_WIDTH), lambda b, i: (b * nq + i, 0)),
        out_shape=jax.ShapeDtypeStruct((batch * n_seq, BRANCH_WIDTH), BF16),
        scratch_shapes=[pltpu.VMEM((GQA_Q_HEADS, LANES, tq), BF16),
                        pltpu.VMEM((GQA_Q_HEADS, 1, tq), F32),
                        pltpu.VMEM((GQA_Q_HEADS, 8, tq), F32),
                        pltpu.VMEM((GQA_Q_HEADS, GQA_VT_ROWS, tq), F32)],
        compiler_params=_params("parallel", "arbitrary"),
        name="gqa_attn",
    )(q, k, vt)


def _diff_kernel(q_ref, k_ref, vt_ref, lam_ref, g_ref, o_ref, qm_sc, m_sc, d_sc, acc_sc, *, lam_init):
    subheads = 2 * DIFF_HEADS
    _prepare_queries(q_ref, k_ref, qm_sc, m_sc, _diff_layout)
    scores = functools.partial(_chunk_scores, k_ref, qm_sc, _diff_layout)

    def values_t(c, j):
        return vt_ref[c, (j // 2) * DIFF_VT_ROWS:(j // 2 + 1) * DIFF_VT_ROWS, :]

    lp = lam_ref[...]
    lam = (jnp.exp(jnp.sum(lp[0:1] * lp[1:2], axis=-1, keepdims=True))
           - jnp.exp(jnp.sum(lp[2:3] * lp[3:4], axis=-1, keepdims=True)) + lam_init)

    def write_out():
        for h in range(DIFF_HEADS):
            a1, a2 = acc_sc[2 * h], acc_sc[2 * h + 1]
            ot = (a1[0:DIFF_VDIM] * (1.0 / a1[DIFF_VDIM:DIFF_VDIM + 1])
                  - a2[0:DIFF_VDIM] * (lam / a2[DIFF_VDIM:DIFF_VDIM + 1]))
            o = _rms(ot.T) * g_ref[...] * (1.0 - lam_init)
            o_ref[:, h * DIFF_VDIM:(h + 1) * DIFF_VDIM] = o.astype(BF16)

    _streaming_softmax_matmul(vt_ref.shape[0], subheads, scores, values_t, m_sc, d_sc, acc_sc,
                              write_out)


def _diff_attention(q, k, vt, lam_params, subln_g, lam_init, batch, n_seq):
    tq = DIFF_Q_TILE
    assert n_seq % tq == 0 and n_seq % TOKEN_TILE == 0
    nq, n_chunks = n_seq // tq, n_seq // TOKEN_TILE
    subheads = 2 * DIFF_HEADS
    return pl.pallas_call(
        functools.partial(_diff_kernel, lam_init=lam_init),
        grid=(batch, nq),
        in_specs=[pl.BlockSpec((tq, BRANCH_WIDTH), lambda b, i: (b * nq + i, 0)),
                  pl.BlockSpec((n_seq, BRANCH_WIDTH), lambda b, i: (b, 0)),
                  pl.BlockSpec((n_chunks, DIFF_HEADS * DIFF_VT_ROWS, TOKEN_TILE),
                               lambda b, i: (b, 0, 0)),
                  _const_spec((4, HEAD_DIM)), _const_spec((1, DIFF_VDIM))],
        out_specs=pl.BlockSpec((tq, BRANCH_WIDTH), lambda b, i: (b * nq + i, 0)),
        out_shape=jax.ShapeDtypeStruct((batch * n_seq, BRANCH_WIDTH), BF16),
        scratch_shapes=[pltpu.VMEM((subheads, LANES, tq), BF16),
                        pltpu.VMEM((subheads, 1, tq), F32),
                        pltpu.VMEM((subheads, 8, tq), F32),
                        pltpu.VMEM((subheads, DIFF_VT_ROWS, tq), F32)],
        compiler_params=_params("parallel", "arbitrary"),
        name="diff_attn",
    )(q, k, vt, lam_params, subln_g)


def _merge_kernel(x_ref, g_ref, oa_ref, ob_ref, oc_ref, wg_ref, wb_ref, wo_ref, y_ref):
    x = x_ref[...]
    h = (_rms(x) * g_ref[...]).astype(BF16)
    merged = None
    for i, o_ref in enumerate((oa_ref, ob_ref, oc_ref)):
        gate = jax.nn.sigmoid(jnp.dot(h, wg_ref[:, i * D_MODEL:(i + 1) * D_MODEL],
                                      preferred_element_type=F32))
        term = gate * jnp.dot(o_ref[...], wb_ref[i], preferred_element_type=F32)
        merged = term if merged is None else merged + term
    y_ref[...] = x + jnp.dot(merged.astype(BF16), wo_ref[...], preferred_element_type=F32)


def _merge(x2d, gain, oa, ob, oc, w_gate, w_branch, w_out):
    t = x2d.shape[0]
    tm = TOKEN_TILE
    tok = lambda c: pl.BlockSpec((tm, c), lambda i: (i, 0))
    return pl.pallas_call(
        _merge_kernel,
        grid=(t // tm,),
        in_specs=[tok(D_MODEL), _const_spec((1, D_MODEL)), tok(BRANCH_WIDTH), tok(BRANCH_WIDTH),
                  tok(BRANCH_WIDTH), _const_spec((D_MODEL, 3 * D_MODEL)),
                  _const_spec((3, BRANCH_WIDTH, D_MODEL)), _const_spec((D_MODEL, D_MODEL))],
        out_specs=tok(D_MODEL),
        out_shape=jax.ShapeDtypeStruct((t, D_MODEL), F32),
        compiler_params=_params("parallel"),
        name="merge",
    )(x2d, gain, oa, ob, oc, w_gate, w_branch, w_out)


FFN_CHUNKS = 1


def _ffn_kernel(x_ref, g_ref, wg_ref, wu_ref, wd_ref, gf_ref, y_ref, *, final_norm):
    x = x_ref[...]
    h = (_rms(x) * g_ref[...]).astype(BF16)
    width = D_FF // FFN_CHUNKS
    y = x
    for c in range(FFN_CHUNKS):
        cs = slice(c * width, (c + 1) * width)
        a = jnp.dot(h, wg_ref[:, cs], preferred_element_type=F32)
        b = jnp.dot(h, wu_ref[:, cs], preferred_element_type=F32)
        act = (a * jax.nn.sigmoid(a) * b).astype(BF16)
        y = y + jnp.dot(act, wd_ref[cs, :], preferred_element_type=F32)
    if final_norm:
        y = _rms(y) * gf_ref[...]
    y_ref[...] = y


def _ffn(x2d, gain, w_gate, w_up, w_down, gain_final, final_norm):
    t = x2d.shape[0]
    tm = TOKEN_TILE
    tok = pl.BlockSpec((tm, D_MODEL), lambda i: (i, 0))
    return pl.pallas_call(
        functools.partial(_ffn_kernel, final_norm=final_norm),
        grid=(t // tm,),
        in_specs=[tok, _const_spec((1, D_MODEL)), _const_spec((D_MODEL, D_FF)),
                  _const_spec((D_MODEL, D_FF)), _const_spec((D_FF, D_MODEL)),
                  _const_spec((1, D_MODEL))],
        out_specs=tok,
        out_shape=jax.ShapeDtypeStruct((t, D_MODEL), F32),
        compiler_params=_params("parallel"),
        name="ffn",
    )(x2d, gain, w_gate, w_up, w_down, gain_final)


def _prepare_layer(l, norm_mix, w_in, na_rpb, qk_norm, diff_lambda, diff_subln, w_branch, w_out,
                   norm_ffn, w_ffn_gate, w_ffn_up, w_ffn_down):
    row = lambda g: g.astype(F32).reshape(1, -1)
    return dict(
        lam_init=0.8 - 0.6 * math.exp(-0.3 * l),
        norm_mix=row(norm_mix[l]),
        w_qkv=_permute_columns(w_in[l][:, :QKV_COLS].astype(BF16)),
        w_gate=w_in[l][:, QKV_COLS:].astype(BF16),
        na_bias=_na_bias_table(na_rpb[l]),
        gq=_gqa_gain_tile(qk_norm[l, 0]),
        gk=_gqa_gain_tile(qk_norm[l, 1]),
        diff_lambda=diff_lambda[l].astype(F32),
        diff_subln=row(diff_subln[l]),
        w_branch=w_branch[l].astype(BF16),
        w_out=w_out[l].astype(BF16),
        norm_ffn=row(norm_ffn[l]),
        w_ffn_gate=w_ffn_gate[l].astype(BF16),
        w_ffn_up=w_ffn_up[l].astype(BF16),
        w_ffn_down=w_ffn_down[l].astype(BF16),
    )


def _trunk(x, layers, norm_final):
    batch, n_seq, _ = x.shape
    x2d = x.reshape(batch * n_seq, D_MODEL)
    tables = _rope_tables(n_seq)
    gf = norm_final.astype(F32).reshape(1, -1)
    for l, p in enumerate(layers):
        qa, ka, va, qb, kb, vb, qc, kc, vc = _inproj(x2d, n_seq, p["norm_mix"], p["w_qkv"], tables,
                                                     p["gq"], p["gk"])
        o_a = _na_attention(qa, ka, va, p["na_bias"], batch, n_seq)
        o_b = _gqa_attention(qb, kb, vb, batch, n_seq)
        o_c = _diff_attention(qc, kc, vc, p["diff_lambda"], p["diff_subln"], p["lam_init"], batch, n_seq)
        x2d = _merge(x2d, p["norm_mix"], o_a, o_b, o_c, p["w_gate"], p["w_branch"], p["w_out"])
        x2d = _ffn(x2d, p["norm_ffn"], p["w_ffn_gate"], p["w_ffn_up"], p["w_ffn_down"], gf,
                   final_norm=(l == len(layers) - 1))
    return x2d.reshape(batch, n_seq, D_MODEL)


def kernel(x_prompt, x_sample, norm_mix, w_in, na_rpb, qk_norm, diff_lambda, diff_subln, w_branch,
           w_out, norm_ffn, w_ffn_gate, w_ffn_up, w_ffn_down, norm_final):
    depth = w_in.shape[0]
    layers = [_prepare_layer(l, norm_mix, w_in, na_rpb, qk_norm, diff_lambda, diff_subln, w_branch,
                             w_out, norm_ffn, w_ffn_gate, w_ffn_up, w_ffn_down) for l in range(depth)]
    return (_trunk(x_prompt, layers, norm_final), _trunk(x_sample, layers, norm_final))
```
